```python
import jax, jax.numpy as jnp
from jax import lax
import numpy as np

D_MODEL = 1024
BATCH = 8
SEQ = 8192
DEPTH = 2

HEAD = 64
RMS_EPS = 1e-6
A_WIDTH = D_MODEL
A_HEADS = A_WIDTH // HEAD
DECAY_LORA = 64
ICL_LORA = 64
GATE_LORA = 128
VRES_LORA = 32
A_COLS = 3 * A_WIDTH + DECAY_LORA + ICL_LORA + GATE_LORA
A_SPLITS = (A_WIDTH, 2 * A_WIDTH, 3 * A_WIDTH, 3 * A_WIDTH + DECAY_LORA, 3 * A_WIDTH + DECAY_LORA + ICL_LORA)
GN_EPS = 64e-5
DILATION_GROUPS = ((128, 1), (512, 4), (2048, 16))
B_HEADS_PER_GROUP = 4
B_HEADS = B_HEADS_PER_GROUP * len(DILATION_GROUPS)
B_WIDTH = B_HEADS * HEAD
B_OUT = B_HEADS_PER_GROUP * HEAD
B_COLS = 3 * B_WIDTH
DIL_BLOCK = 128
ROPE_THETA = 10000.0
C_WIDTH = D_MODEL
C_GROUPS = 8
C_CHUNK = 128
C_COLS = 2 * C_WIDTH
LN_EPS = 1e-5
N_BRANCHES = 3
GATE_COLS = N_BRANCHES * D_MODEL
IN_COLS = A_COLS + B_COLS + C_COLS + GATE_COLS
IN_SPLITS = (A_COLS, A_COLS + B_COLS, A_COLS + B_COLS + C_COLS)
N_GROUPS = 4
EXPERTS_PER_GROUP = 8
N_EXPERTS = N_GROUPS * EXPERTS_PER_GROUP
TOP_K = 2
EXPERT_FF = D_MODEL // 2
MOE_BLOCK = 256

kernel_name = "hybrid_rwkv7_dilated_gmlp_hiermoe"

f32 = jnp.float32


def rms_norm(x, g):
    xf = x.astype(f32)
    y = xf * lax.rsqrt(jnp.mean(xf * xf, axis=-1, keepdims=True) + RMS_EPS)
    return y.astype(x.dtype) * g


def token_shift_lerp(z, mu):
    prev = jnp.pad(z, ((0, 0), (1, 0), (0, 0)))[:, :-1]
    return z + (prev - z) * mu


def wkv7_scan(r, decay, k, v, a, b):
    Bsz, T, H, N = r.shape

    def step(S, inp):
        r_t, w_t, k_t, v_t, a_t, b_t = inp
        Sa = jnp.einsum('bhij,bhj->bhi', S, a_t)
        S = S * w_t[:, :, None, :] + Sa[..., None] * b_t[:, :, None, :] + v_t[..., None] * k_t[:, :, None, :]
        return S, jnp.einsum('bhij,bhj->bhi', S, r_t)

    seq = tuple(jnp.moveaxis(z.astype(f32), 1, 0) for z in (r, decay, k, v, a, b))
    _, y = lax.scan(step, jnp.zeros((Bsz, H, N, N), f32), seq)
    return jnp.moveaxis(y, 0, 1)


def rwkv7_time_mix(r, k, v, w_lo, a_lo, g_lo, w0, w2, a0, a2, g2, k_k, k_a, r_k, lnx_w, lnx_b):
    Bsz, T, _ = r.shape
    heads = lambda z: z.reshape(Bsz, T, A_HEADS, HEAD)
    w = -jax.nn.softplus(-(w0 + jnp.tanh(w_lo) @ w2).astype(f32)) - 0.5
    decay = jnp.exp(-jnp.exp(w))
    a = jax.nn.sigmoid((a0 + a_lo @ a2).astype(f32))
    g = jax.nn.sigmoid(g_lo) @ g2
    kk = heads((k * k_k).astype(f32))
    kk = kk / jnp.maximum(jnp.sqrt(jnp.sum(kk * kk, axis=-1, keepdims=True)), 1e-12)
    k = k.astype(f32) * (1.0 + (a - 1.0) * k_a.astype(f32))
    rh, kh, vh, ah = heads(r.astype(f32)), heads(k), heads(v.astype(f32)), heads(a)
    y = wkv7_scan(rh, heads(decay), kh, vh, -kk, kk * ah)
    mean = jnp.mean(y, axis=-1, keepdims=True)
    var = jnp.mean(jnp.square(y - mean), axis=-1, keepdims=True)
    y = (y - mean) * lax.rsqrt(var + GN_EPS)
    y = y.reshape(Bsz, T, A_WIDTH) * lnx_w.astype(f32) + lnx_b.astype(f32)
    bonus = jnp.sum(rh * kh * r_k.astype(f32), axis=-1, keepdims=True) * vh
    y = y + bonus.reshape(Bsz, T, A_WIDTH)
    return (y * g.astype(f32)).astype(r.dtype)


def rotary(z, cos, sin):
    z1, z2 = jnp.split(z.astype(f32), 2, axis=-1)
    c, s = cos[:, None, :], sin[:, None, :]
    return jnp.concatenate([z1 * c - z2 * s, z1 * s + z2 * c], axis=-1).astype(z.dtype)


def dilated_group_attention(q, k, v, dilation, span):
    Bsz, T, Hg, hd = q.shape
    L = T // dilation
    nb = -(-L // DIL_BLOCK)
    Lp = nb * DIL_BLOCK

    def strided(z):
        z = z.reshape(Bsz, L, dilation, Hg, hd).transpose(0, 2, 3, 1, 4)
        return jnp.pad(z, ((0, 0), (0, 0), (0, 0), (0, Lp - L), (0, 0)))

    def windows(z):
        z = jnp.pad(strided(z), ((0, 0), (0, 0), (0, 0), (DIL_BLOCK, 0), (0, 0)))
        z = z.reshape(Bsz, dilation, Hg, nb + 1, DIL_BLOCK, hd)
        return jnp.concatenate([z[:, :, :, :-1], z[:, :, :, 1:]], axis=-2)

    qs = strided(q).reshape(Bsz, dilation, Hg, nb, DIL_BLOCK, hd).astype(f32)
    ks, vs = windows(k).astype(f32), windows(v).astype(f32)
    s = jnp.einsum('bdhnqc,bdhnkc->bdhnqk', qs, ks) * (hd ** -0.5)
    qi = jnp.arange(DIL_BLOCK)[:, None]
    kj = jnp.arange(2 * DIL_BLOCK)[None, :]
    dist = qi + DIL_BLOCK - kj
    key_idx = jnp.arange(nb)[:, None, None] * DIL_BLOCK - DIL_BLOCK + kj
    mask = (dist >= 0) & (dist <= span) & (key_idx >= 0)
    s = jnp.where(mask, s, -jnp.inf)
    m = jnp.max(s, axis=-1, keepdims=True)
    e = jnp.exp(s - m)
    den = jnp.sum(e, axis=-1, keepdims=True)
    o = jnp.einsum('bdhnqk,bdhnkc->bdhnqc', e, vs) / den
    lse = (m + jnp.log(den))[..., 0]

    def unstride(z):
        z = z.reshape(Bsz, dilation, Hg, Lp, *z.shape[5:])[:, :, :, :L]
        z = jnp.moveaxis(z, 3, 1)
        return z.reshape(Bsz, T, Hg, *z.shape[4:])

    return unstride(o), unstride(lse)


def dilated_attention(q, k, v):
    outs, lses = [], []
    for gi, (window, dilation) in enumerate(DILATION_GROUPS):
        sl = slice(gi * B_HEADS_PER_GROUP, (gi + 1) * B_HEADS_PER_GROUP)
        o, lse = dilated_group_attention(q[:, :, sl], k[:, :, sl], v[:, :, sl], dilation, window // dilation)
        outs.append(o)
        lses.append(lse)
    wts = jax.nn.softmax(jnp.stack(lses, axis=0), axis=0)
    o = jnp.einsum('gbth,gbthc->bthc', wts, jnp.stack(outs, axis=0))
    Bsz, T = q.shape[:2]
    return o.reshape(Bsz, T, B_OUT).astype(q.dtype)


def spatial_gating(zc, ln_w, ln_b, ws, bs):
    z = jax.nn.gelu(zc, approximate=False)
    u, v = jnp.split(z, 2, axis=-1)
    vf = v.astype(f32)
    mean = jnp.mean(vf, axis=-1, keepdims=True)
    var = jnp.mean(jnp.square(vf - mean), axis=-1, keepdims=True)
    v = ((vf - mean) * lax.rsqrt(var + LN_EPS)).astype(v.dtype) * ln_w + ln_b
    Bsz, T, _ = v.shape
    vg = v.reshape(Bsz, T // C_CHUNK, C_CHUNK, C_GROUPS, C_WIDTH // C_GROUPS)
    ws_causal = ws * jnp.tril(jnp.ones((C_CHUNK, C_CHUNK), ws.dtype))
    mixed = jnp.einsum('gts,bnsgc->bntgc', ws_causal, vg) + bs.T[:, :, None]
    return u * mixed.reshape(Bsz, T, C_WIDTH)


def hierarchical_moe(h, rg, rgb, re, reb, w1, w3, w2):
    Bsz, T, D = h.shape
    N = Bsz * T
    NK = N * TOP_K
    xf = h.reshape(N, D)
    glog = (xf @ rg + rgb).astype(f32)
    gprob = jax.nn.softmax(glog, axis=-1)
    gsel = jnp.argmax(glog, axis=-1)
    p_g = jnp.take_along_axis(gprob, gsel[:, None], axis=1)[:, 0]
    elog = (xf @ re + reb).astype(f32).reshape(N, N_GROUPS, EXPERTS_PER_GROUP)
    elog_g = jnp.take_along_axis(elog, gsel[:, None, None], axis=1)[:, 0]
    top_v, top_i = lax.top_k(elog_g, TOP_K)
    wts = jax.nn.softmax(top_v, axis=-1) * p_g[:, None]
    eid = gsel[:, None] * EXPERTS_PER_GROUP + top_i
    flat_e = eid.reshape(-1).astype(jnp.int32)
    flat_t = jnp.repeat(jnp.arange(N, dtype=jnp.int32), TOP_K)
    flat_w = wts.reshape(-1)
    order = jnp.argsort(flat_e)
    se, st, sw = flat_e[order], flat_t[order], flat_w[order]
    counts = jnp.bincount(flat_e, length=N_EXPERTS)
    padded = (counts + MOE_BLOCK - 1) // MOE_BLOCK * MOE_BLOCK
    pend = jnp.cumsum(padded)
    pstart = pend - padded
    cstart = jnp.cumsum(counts) - counts
    dest = pstart[se] + jnp.arange(NK, dtype=jnp.int32) - cstart[se]
    nblk = -(-NK // MOE_BLOCK) + N_EXPERTS
    P = nblk * MOE_BLOCK
    tok_buf = jnp.full((P,), N, jnp.int32).at[dest].set(st)
    w_buf = jnp.zeros((P,), f32).at[dest].set(sw)
    blk_e = jnp.minimum(jnp.searchsorted(pend, jnp.arange(nblk) * MOE_BLOCK, side='right'), N_EXPERTS - 1)
    xpad = jnp.concatenate([xf, jnp.zeros((1, D), xf.dtype)], axis=0)
    xb = xpad[tok_buf].reshape(nblk, MOE_BLOCK, D)

    def expert_block(args):
        xblk, e = args
        hid = jax.nn.silu(xblk @ w1[e]) * (xblk @ w3[e])
        return hid @ w2[e]

    yb = lax.map(expert_block, (xb, blk_e)).reshape(P, D)
    y = jax.ops.segment_sum(yb * w_buf[:, None].astype(yb.dtype), tok_buf, num_segments=N + 1)[:N]
    return y.reshape(Bsz, T, D)


def setup_inputs(seed: int = 0) -> dict:
    key = jax.random.key(seed)
    ks = iter(jax.random.split(key, 48))
    nrm = lambda shape, scale: scale * jax.random.normal(next(ks), shape, f32)
    unif = lambda shape, lo, hi: jax.random.uniform(next(ks), shape, f32, lo, hi)
    L, D = DEPTH, D_MODEL
    return {
        "x": nrm((BATCH, SEQ, D), 1.0),
        "norm_mix": 1.0 + nrm((L, D), 0.02),
        "w_in": nrm((L, D, IN_COLS), D ** -0.5),
        "mu_a": unif((L, A_COLS), 0.0, 1.0),
        "w_decay0": unif((L, A_WIDTH), -6.0, -1.0),
        "w_decay2": nrm((L, DECAY_LORA, A_WIDTH), 0.1),
        "a0": nrm((L, A_WIDTH), 0.1),
        "a2": nrm((L, ICL_LORA, A_WIDTH), 0.5 * ICL_LORA ** -0.5),
        "g2": nrm((L, GATE_LORA, A_WIDTH), GATE_LORA ** -0.5),
        "k_k": 0.85 + nrm((L, A_WIDTH), 0.02),
        "k_a": 1.0 + nrm((L, A_WIDTH), 0.02),
        "r_k": nrm((L, A_HEADS, HEAD), 0.1),
        "lnx_w": 1.0 + nrm((L, A_WIDTH), 0.02),
        "lnx_b": nrm((L, A_WIDTH), 0.02),
        "vres_mu": unif((L - 1, VRES_LORA), 0.0, 1.0),
        "vres_w1": nrm((L - 1, D, VRES_LORA), D ** -0.5),
        "vres_w2": nrm((L - 1, VRES_LORA, A_WIDTH), 0.5 * VRES_LORA ** -0.5),
        "vres_b": 1.0 + nrm((L - 1, A_WIDTH), 0.1),
        "sgu_ln_w": 1.0 + nrm((L, C_WIDTH), 0.02),
        "sgu_ln_b": nrm((L, C_WIDTH), 0.02),
        "sgu_ws": nrm((L, C_GROUPS, C_CHUNK, C_CHUNK), C_CHUNK ** -0.5),
        "sgu_bs": 1.0 + nrm((L, C_GROUPS, C_CHUNK), 0.02),
        "gate_b": nrm((L, N_BRANCHES, D), 0.1),
        "w_pa": nrm((L, A_WIDTH, D), A_WIDTH ** -0.5),
        "w_pb": nrm((L, B_OUT, D), B_OUT ** -0.5),
        "w_pc": nrm((L, C_WIDTH, D), C_WIDTH ** -0.5),
        "w_out": nrm((L, D, D), D ** -0.5),
        "norm_ffn": 1.0 + nrm((L, D), 0.02),
        "router_grp": nrm((L, D, N_GROUPS), D ** -0.5),
        "router_grp_b": nrm((L, N_GROUPS), 0.01),
        "router_exp": nrm((L, D, N_EXPERTS), D ** -0.5),
        "router_exp_b": nrm((L, N_EXPERTS), 0.01),
        "moe_w1": nrm((L, N_EXPERTS, D, EXPERT_FF), D ** -0.5),
        "moe_w3": nrm((L, N_EXPERTS, D, EXPERT_FF), D ** -0.5),
        "moe_w2": nrm((L, N_EXPERTS, EXPERT_FF, D), EXPERT_FF ** -0.5),
        "norm_final": 1.0 + nrm((D,), 0.02),
    }


def reference(x, norm_mix, w_in, mu_a, w_decay0, w_decay2, a0, a2, g2, k_k, k_a, r_k, lnx_w, lnx_b,
              vres_mu, vres_w1, vres_w2, vres_b, sgu_ln_w, sgu_ln_b, sgu_ws, sgu_bs, gate_b,
              w_pa, w_pb, w_pc, w_out, norm_ffn, router_grp, router_grp_b, router_exp, router_exp_b,
              moe_w1, moe_w3, moe_w2, norm_final):
    Bsz, T, D = x.shape
    pos = jnp.arange(T, dtype=f32)
    inv_freq = ROPE_THETA ** (-jnp.arange(0, HEAD, 2, dtype=f32) / HEAD)
    ang = pos[:, None] * inv_freq[None, :]
    cos, sin = jnp.cos(ang), jnp.sin(ang)
    v_first = None
    for l in range(DEPTH):
        h = rms_norm(x, norm_mix[l])
        proj = h @ w_in[l]
        za, zb, zc, zg = jnp.split(proj, IN_SPLITS, axis=-1)
        za = token_shift_lerp(za, mu_a[l])
        r, k, v, w_lo, a_lo, g_lo = jnp.split(za, A_SPLITS, axis=-1)
        if l == 0:
            v_first = v
        else:
            vlo = token_shift_lerp(h @ vres_w1[l - 1], vres_mu[l - 1])
            v = v + (v_first - v) * jax.nn.sigmoid(vres_b[l - 1] + vlo @ vres_w2[l - 1])
        ya = rwkv7_time_mix(r, k, v, w_lo, a_lo, g_lo, w_decay0[l], w_decay2[l], a0[l], a2[l], g2[l],
                            k_k[l], k_a[l], r_k[l], lnx_w[l], lnx_b[l])
        qb, kb, vb = (z.reshape(Bsz, T, B_HEADS, HEAD) for z in jnp.split(zb, 3, axis=-1))
        yb = dilated_attention(rotary(qb, cos, sin), rotary(kb, cos, sin), vb)
        yc = spatial_gating(zc, sgu_ln_w[l], sgu_ln_b[l], sgu_ws[l], sgu_bs[l])
        gates = jax.nn.sigmoid(zg.reshape(Bsz, T, N_BRANCHES, D) + gate_b[l])
        merged = (gates[:, :, 0] * (ya @ w_pa[l]) + gates[:, :, 1] * (yb @ w_pb[l])
                  + gates[:, :, 2] * (yc @ w_pc[l]))
        x = x + merged @ w_out[l]
        x = x + hierarchical_moe(rms_norm(x, norm_ffn[l]), router_grp[l], router_grp_b[l], router_exp[l],
                                 router_exp_b[l], moe_w1[l], moe_w3[l], moe_w2[l])
    return rms_norm(x, norm_final)
```

```python
import functools
import math

import jax
import jax.numpy as jnp
from jax import lax
from jax.experimental import pallas as pl
from jax.experimental.pallas import tpu as pltpu

f32 = jnp.float32
bf16 = jnp.bfloat16
i32 = jnp.int32

D_MODEL = 1024
HEAD = 64
RMS_EPS = 1e-6
GN_EPS = 64e-5
LN_EPS = 1e-5
ROPE_THETA = 10000.0
A_WIDTH = D_MODEL
A_COLS = 3 * A_WIDTH + 64 + 64 + 128
B_COLS = 3 * 768
C_COLS = 2 * D_MODEL
G_COLS = 3 * D_MODEL
DILATIONS = (1, 4, 16)
ATT_BLOCK = 128
ATT_SPAN = 128
SGU_CHUNK = 128
SGU_GROUPS = 8
N_GROUPS = 4
EXPERTS_PER_GROUP = 8
N_EXPERTS = N_GROUPS * EXPERTS_PER_GROUP
EXPERT_FF = D_MODEL // 2
MOE_BLOCK = 256
WKV_CHUNK = 128
LANES = 128
VMEM_LIMIT = 48 * 1024 * 1024


def _cparams(*sem):
    return pltpu.CompilerParams(dimension_semantics=sem, vmem_limit_bytes=VMEM_LIMIT)


def _bdot(a, b):
    return jnp.dot(a.astype(bf16), b.astype(bf16), preferred_element_type=f32)


def _dot_hilo(a, b):
    hi = a.astype(bf16)
    lo = (a - hi.astype(f32)).astype(bf16)
    return (jnp.dot(hi, b, preferred_element_type=f32)
            + jnp.dot(lo, b, preferred_element_type=f32))


def _head_sum(x, bd):
    parts = [_dot_hilo(x[:, p * LANES:(p + 1) * LANES], bd) for p in range(x.shape[1] // LANES)]
    return jnp.concatenate(parts, axis=1)


def _sigmoid(x):
    return 1.0 / (1.0 + jnp.exp(-x))


def _rmsnorm_kernel(x_ref, g_ref, o_ref):
    x = x_ref[...]
    y = x * lax.rsqrt(jnp.mean(x * x, axis=-1, keepdims=True) + RMS_EPS)
    o_ref[...] = (y * g_ref[...]).astype(o_ref.dtype)


def _rmsnorm(x, g, out_dtype, tm=1024):
    n, d = x.shape
    return pl.pallas_call(
        _rmsnorm_kernel,
        out_shape=jax.ShapeDtypeStruct((n, d), out_dtype),
        grid=(n // tm,),
        in_specs=[pl.BlockSpec((tm, d), lambda i: (i, 0)), pl.BlockSpec((1, d), lambda i: (0, 0))],
        out_specs=pl.BlockSpec((tm, d), lambda i: (i, 0)),
        compiler_params=_cparams("parallel"),
        name="rmsnorm",
    )(x, g.reshape(1, d))


def _matmul_kernel(h_ref, w_ref, o_ref):
    o_ref[...] = jnp.dot(h_ref[...], w_ref[...], preferred_element_type=f32).astype(o_ref.dtype)


def _proj(h, w, tn, name, tm=1024):
    n, k = h.shape
    m = w.shape[1]
    return pl.pallas_call(
        _matmul_kernel,
        out_shape=jax.ShapeDtypeStruct((n, m), bf16),
        grid=(m // tn, n // tm),
        in_specs=[pl.BlockSpec((tm, k), lambda j, i: (i, 0)), pl.BlockSpec((k, tn), lambda j, i: (0, j))],
        out_specs=pl.BlockSpec((tm, tn), lambda j, i: (i, j)),
        compiler_params=_cparams("parallel", "parallel"),
        name=name,
    )(h, w)


def _wkv_prep_kernel(has_vres, tm, *refs):
    if has_vres:
        (za_ref, h_ref, vfirst_ref, mu_ref, w0_ref, w2_ref, a0_ref, a2_ref, g2_ref, kk_ref, ka_ref,
         rk_ref, bd_ref, vw1_ref, vmu_ref, vw2_ref, vb_ref,
         rt_ref, at_ref, v_ref, bT_ref, kT_ref, pcT_ref, g_ref, bonus_ref, carry, carry_v) = refs
    else:
        (za_ref, mu_ref, w0_ref, w2_ref, a0_ref, a2_ref, g2_ref, kk_ref, ka_ref, rk_ref, bd_ref,
         rt_ref, at_ref, v_ref, bT_ref, kT_ref, pcT_ref, g_ref, bonus_ref, carry) = refs
    t = pl.program_id(1)

    @pl.when(t == 0)
    def _():
        carry[...] = jnp.zeros_like(carry)
        if has_vres:
            carry_v[...] = jnp.zeros_like(carry_v)

    row = lax.broadcasted_iota(i32, (tm, 1), 0)
    z = za_ref[...].astype(f32)
    prev = jnp.where(row == 0, carry[0:1, :], pltpu.roll(z, 1, axis=0))
    carry[0:1, :] = z[tm - 1:tm, :]
    zs = z + (prev - z) * mu_ref[...]
    w = A_WIDTH
    r, k, v = zs[:, 0:w], zs[:, w:2 * w], zs[:, 2 * w:3 * w]
    lo = zs[:, 3 * w:3 * w + LANES]
    glo = zs[:, 3 * w + LANES:3 * w + 2 * LANES]

    if has_vres:
        hv = jnp.dot(h_ref[...], vw1_ref[...], preferred_element_type=f32)
        prev_v = jnp.where(row == 0, carry_v[0:1, :], pltpu.roll(hv, 1, axis=0))
        carry_v[0:1, :] = hv[tm - 1:tm, :]
        vlo = hv + (prev_v - hv) * vmu_ref[...]
        sg = _sigmoid(vb_ref[...] + _bdot(vlo, vw2_ref[...]))
        v = v + (vfirst_ref[...].astype(f32) - v) * sg

    wl = w0_ref[...] + _bdot(jnp.tanh(lo), w2_ref[...])
    yy = -wl
    softplus = jnp.maximum(yy, 0.0) + jnp.log(1.0 + jnp.exp(-jnp.abs(yy)))
    ld = -jnp.exp(-softplus - 0.5)
    a = _sigmoid(a0_ref[...] + _bdot(lo, a2_ref[...]))
    g = _bdot(_sigmoid(glo), g2_ref[...])
    bd = bd_ref[...]
    kk = k * kk_ref[...]
    kk = kk / jnp.maximum(jnp.sqrt(_head_sum(kk * kk, bd)), 1e-12)
    kp = k * (1.0 + (a - 1.0) * ka_ref[...])
    bonus = _head_sum(r * kp * rk_ref[...], bd) * v

    tloc = row % WKV_CHUNK
    cum = ld
    s = 1
    while s < WKV_CHUNK:
        cum = cum + jnp.where(tloc >= s, pltpu.roll(cum, s, axis=0), 0.0)
        s *= 2
    p_inc = jnp.exp(cum)
    p_inv = jnp.exp(-cum)
    ends = [cum[(c + 1) * WKV_CHUNK - 1:(c + 1) * WKV_CHUNK, :] for c in range(tm // WKV_CHUNK)]
    cum_end = ends[0]
    for c in range(1, tm // WKV_CHUNK):
        cum_end = jnp.where(row >= c * WKV_CHUNK, ends[c], cum_end)
    cum_end = jnp.broadcast_to(cum_end, (tm, w))

    rt_ref[...] = (r * p_inc).astype(bf16)
    at_ref[...] = (-kk * jnp.exp(cum - ld)).astype(bf16)
    v_ref[...] = v.astype(bf16)
    bT_ref[...] = (kk * a * p_inv).T.astype(bf16)
    kT_ref[...] = (kp * p_inv).T.astype(bf16)
    pcT_ref[...] = jnp.exp(cum_end).T
    g_ref[...] = g.astype(bf16)
    bonus_ref[...] = bonus.astype(bf16)


def _wkv_prep(za, bsz, T, p, vres, tm=256):
    has_vres = vres is not None
    w = A_WIDTH
    tok = lambda cols: pl.BlockSpec((None, tm, cols), lambda b, t: (b, t, 0))
    par = lambda r_, c_: pl.BlockSpec((r_, c_), lambda b, t: (0, 0))
    ins = [za.reshape(bsz, T, A_COLS)]
    specs = [tok(A_COLS)]
    if has_vres:
        ins += [vres["h"].reshape(bsz, T, D_MODEL), vres["vfirst"]]
        specs += [tok(D_MODEL), tok(w)]
    ins += [p["mu"], p["w0"], p["w2"], p["a0"], p["a2"], p["g2"], p["k_k"], p["k_a"], p["r_k"], p["bd"]]
    specs += [par(1, A_COLS), par(1, w), par(LANES, w), par(1, w), par(LANES, w), par(LANES, w),
              par(1, w), par(1, w), par(1, w), par(LANES, LANES)]
    if has_vres:
        ins += [vres["w1"], vres["mu"], vres["w2"], vres["b"]]
        specs += [par(D_MODEL, LANES), par(1, LANES), par(LANES, w), par(1, w)]
    tr = pl.BlockSpec((None, w, tm), lambda b, t: (b, 0, t))
    out_shape = [jax.ShapeDtypeStruct((bsz, T, w), bf16)] * 3 + \
                [jax.ShapeDtypeStruct((bsz, w, T), bf16)] * 2 + \
                [jax.ShapeDtypeStruct((bsz, w, T), f32)] + \
                [jax.ShapeDtypeStruct((bsz, T, w), bf16)] * 2
    out_specs = [tok(w)] * 3 + [tr] * 3 + [tok(w)] * 2
    scratch = [pltpu.VMEM((8, A_COLS), f32)]
    if has_vres:
        scratch.append(pltpu.VMEM((8, LANES), f32))
    return pl.pallas_call(
        functools.partial(_wkv_prep_kernel, has_vres, tm),
        out_shape=out_shape,
        grid=(bsz, T // tm),
        in_specs=specs,
        out_specs=out_specs,
        scratch_shapes=scratch,
        compiler_params=_cparams("parallel", "arbitrary"),
        name="wkv_prep",
    )(*ins)


def _wkv_scan_kernel(rt_ref, at_ref, v_ref, bT_ref, kT_ref, pcT_ref, y_ref, st_ref):
    C = WKV_CHUNK

    @pl.when(pl.program_id(1) == 0)
    def _():
        st_ref[...] = jnp.zeros_like(st_ref)

    lane = lax.broadcasted_iota(i32, (1, LANES), 1)
    head0 = lane < HEAD
    key0 = lax.broadcasted_iota(i32, (LANES, 1), 0) < HEAD
    ri = lax.broadcasted_iota(i32, (2 * C, 2 * C), 0) % C
    ci = lax.broadcasted_iota(i32, (2 * C, 2 * C), 1) % C
    strict = ri > ci
    lower = ri >= ci
    eye = (lax.broadcasted_iota(i32, (2 * C, 2 * C), 0)
           == lax.broadcasted_iota(i32, (2 * C, 2 * C), 1)).astype(f32)

    def stack_rows(x):
        return jnp.concatenate([jnp.where(head0, x, 0), jnp.where(head0, 0, x)], axis=0)

    def stack_cols(xT):
        return jnp.concatenate([jnp.where(key0, xT, 0), jnp.where(key0, 0, xT)], axis=1)

    def pair(pi, carry_):
        c0 = pl.multiple_of(pi * LANES, LANES)
        ah = stack_rows(at_ref[:, pl.ds(c0, LANES)])
        rh = stack_rows(rt_ref[:, pl.ds(c0, LANES)])
        vh = stack_rows(v_ref[:, pl.ds(c0, LANES)])
        bkT = jnp.concatenate([stack_cols(bT_ref[pl.ds(c0, LANES), :]),
                               stack_cols(kT_ref[pl.ds(c0, LANES), :])], axis=1)
        pc = pcT_ref[pl.ds(c0, LANES), :]
        gram = jnp.dot(jnp.concatenate([ah, rh], axis=0), bkT, preferred_element_type=f32)
        n_ab = jnp.where(strict, gram[0:2 * C, 0:2 * C], 0.0)
        a_ak = jnp.where(strict, gram[0:2 * C, 2 * C:4 * C], 0.0).astype(bf16)
        a_rb = jnp.where(lower, gram[2 * C:4 * C, 0:2 * C], 0.0).astype(bf16)
        a_rk = jnp.where(lower, gram[2 * C:4 * C, 2 * C:4 * C], 0.0).astype(bf16)
        winv = eye + n_ab
        npow = n_ab
        s = 2
        while s < C:
            npow = _bdot(npow, npow)
            winv = winv + _bdot(winv, npow)
            s *= 2
        st = st_ref[pi]
        st_b = st.astype(bf16)
        xin = jnp.dot(ah, st_b, preferred_element_type=f32) + jnp.dot(a_ak, vh, preferred_element_type=f32)
        u = _bdot(winv, xin)
        u_b = u.astype(bf16)
        yh = (jnp.dot(rh, st_b, preferred_element_type=f32)
              + jnp.dot(a_rb, u_b, preferred_element_type=f32)
              + jnp.dot(a_rk, vh, preferred_element_type=f32))
        y_ref[:, pl.ds(c0, LANES)] = yh[0:C] + yh[C:2 * C]
        pc4 = jnp.concatenate([pc] * 4, axis=1)
        bkp = (bkT.astype(f32) * pc4).astype(bf16)
        upd = jnp.dot(bkp, jnp.concatenate([u_b, vh], axis=0), preferred_element_type=f32)
        st_ref[pi] = st * pc + upd
        return carry_

    lax.fori_loop(0, A_WIDTH // LANES, pair, 0)


def _wkv_scan(rt, at, v, bT, kT, pcT, bsz, T):
    C = WKV_CHUNK
    w = A_WIDTH
    tok = pl.BlockSpec((None, C, w), lambda b, t: (b, t, 0))
    tr = pl.BlockSpec((None, w, C), lambda b, t: (b, 0, t))
    return pl.pallas_call(
        _wkv_scan_kernel,
        out_shape=jax.ShapeDtypeStruct((bsz, T, w), f32),
        grid=(bsz, T // C),
        in_specs=[tok, tok, tok, tr, tr, tr],
        out_specs=tok,
        scratch_shapes=[pltpu.VMEM((w // LANES, LANES, LANES), f32)],
        compiler_params=_cparams("parallel", "arbitrary"),
        name="wkv_scan",
    )(rt, at, v, bT, kT, pcT)


def _attn_kernel(q_ref, kc_ref, kp_ref, vc_ref, vp_ref, cq_ref, sq_ref, cp_ref, sp_ref, o_ref, lse_ref):
    n = pl.program_id(2)
    blk = ATT_BLOCK
    width = q_ref.shape[-1]
    lane = lax.broadcasted_iota(i32, (1, width), 1)
    first_half = (lane % HEAD) < (HEAD // 2)

    def rope(z_ref, c_ref, s_ref):
        z = z_ref[...].astype(f32)
        partner = jnp.where(first_half, pltpu.roll(z, width - HEAD // 2, axis=1),
                            pltpu.roll(z, HEAD // 2, axis=1))
        return z * c_ref[...] + partner * s_ref[...]

    q = rope(q_ref, cq_ref, sq_ref) * (HEAD ** -0.5)
    kcat = jnp.concatenate([rope(kp_ref, cp_ref, sp_ref), rope(kc_ref, cq_ref, sq_ref)], axis=0).astype(bf16)
    vcat = jnp.concatenate([vp_ref[...], vc_ref[...]], axis=0)
    qi = lax.broadcasted_iota(i32, (blk, 2 * blk), 0)
    kj = lax.broadcasted_iota(i32, (blk, 2 * blk), 1)
    dist = qi + blk - kj
    mask = (dist >= 0) & (dist <= ATT_SPAN) & ((kj >= blk) | (n > 0))
    o_acc = jnp.zeros((blk, width), f32)
    lse_acc = jnp.zeros((blk, width), f32)
    for h in range(width // HEAD):
        hm = (lane // HEAD) == h
        qh = jnp.where(hm, q, 0.0).astype(bf16)
        s = lax.dot_general(qh, kcat, (((1,), (1,)), ((), ())), preferred_element_type=f32)
        s = jnp.where(mask, s, -jnp.inf)
        m = jnp.max(s, axis=-1, keepdims=True)
        e = jnp.exp(s - m)
        den = jnp.sum(e, axis=-1, keepdims=True)
        oh = jnp.dot(e.astype(bf16), vcat, preferred_element_type=f32) / den
        o_acc = jnp.where(hm, oh, o_acc)
        lse_acc = jnp.where(hm, m + jnp.log(den), lse_acc)
    o_ref[...] = o_acc.astype(o_ref.dtype)
    lse_ref[...] = lse_acc


def _attention_group(zb, rope_tab, bsz, T, gi, dil):
    blk = ATT_BLOCK
    L = T // dil
    nb = L // blk
    width = 4 * HEAD
    cb = B_COLS // width
    zv = zb.reshape(bsz, L, dil * B_COLS)
    tab = rope_tab.reshape(L, dil * 2 * width)
    cur = lambda off: pl.BlockSpec((None, blk, width), lambda b, r, n: (b, n, r * cb + off + gi))
    prv = lambda off: pl.BlockSpec((None, blk, width), lambda b, r, n: (b, jnp.maximum(n - 1, 0), r * cb + off + gi))
    tcur = lambda off: pl.BlockSpec((blk, width), lambda b, r, n: (n, 2 * r + off))
    tprv = lambda off: pl.BlockSpec((blk, width), lambda b, r, n: (jnp.maximum(n - 1, 0), 2 * r + off))
    ospec = pl.BlockSpec((None, blk, width), lambda b, r, n: (b, n, r))
    o, lse = pl.pallas_call(
        _attn_kernel,
        out_shape=[jax.ShapeDtypeStruct((bsz, L, dil * width), bf16),
                   jax.ShapeDtypeStruct((bsz, L, dil * width), f32)],
        grid=(bsz, dil, nb),
        in_specs=[cur(0), cur(3), prv(3), cur(6), prv(6), tcur(0), tcur(1), tprv(0), tprv(1)],
        out_specs=[ospec, ospec],
        compiler_params=_cparams("parallel", "parallel", "arbitrary"),
        name=f"dilated_attn_d{dil}",
    )(zv, zv, zv, zv, zv, tab, tab, tab, tab)
    return o.reshape(bsz * T, width), lse.reshape(bsz * T, width)


def _sgu_kernel(tm, zc_ref, lnw_ref, lnb_ref, ws_ref, bs_ref, o_ref):
    ch = SGU_CHUNK
    z = zc_ref[...].astype(f32)
    ge = 0.5 * z * (1.0 + lax.erf(z * (2.0 ** -0.5)))
    u = ge[:, :D_MODEL]
    v = ge[:, D_MODEL:]
    mean = jnp.mean(v, axis=-1, keepdims=True)
    var = jnp.mean(jnp.square(v - mean), axis=-1, keepdims=True)
    vn = ((v - mean) * lax.rsqrt(var + LN_EPS) * lnw_ref[...] + lnb_ref[...]).astype(bf16)
    ri = lax.broadcasted_iota(i32, (ch, ch), 0)
    ci = lax.broadcasted_iota(i32, (ch, ch), 1)
    causal = ri >= ci
    for g in range(SGU_GROUPS):
        wg = jnp.where(causal, ws_ref[g], 0.0).astype(bf16)
        bias = bs_ref[g]
        for c in range(tm // ch):
            rows = slice(c * ch, (c + 1) * ch)
            cols = slice(g * LANES, (g + 1) * LANES)
            mixed = jnp.dot(wg, vn[rows, cols], preferred_element_type=f32) + bias
            o_ref[rows, cols] = (u[rows, cols] * mixed).astype(o_ref.dtype)


def _sgu(zc, lnw, lnb, ws, bs_b, tm=512):
    n = zc.shape[0]
    return pl.pallas_call(
        functools.partial(_sgu_kernel, tm),
        out_shape=jax.ShapeDtypeStruct((n, D_MODEL), bf16),
        grid=(n // tm,),
        in_specs=[pl.BlockSpec((tm, C_COLS), lambda i: (i, 0)),
                  pl.BlockSpec((1, D_MODEL), lambda i: (0, 0)),
                  pl.BlockSpec((1, D_MODEL), lambda i: (0, 0)),
                  pl.BlockSpec((SGU_GROUPS, SGU_CHUNK, SGU_CHUNK), lambda i: (0, 0, 0)),
                  pl.BlockSpec((SGU_GROUPS, SGU_CHUNK, LANES), lambda i: (0, 0, 0))],
        out_specs=pl.BlockSpec((tm, D_MODEL), lambda i: (i, 0)),
        compiler_params=_cparams("parallel"),
        name="sgu",
    )(zc, lnw, lnb, ws, bs_b)


def _merge_kernel(x_ref, y_ref, g_ref, bonus_ref, o0_ref, o1_ref, o2_ref, l0_ref, l1_ref, l2_ref,
                  yc_ref, zg_ref, lnxw_ref, lnxb_ref, gb_ref, wpa_ref, wpb_ref, wpc_ref, wout_ref,
                  nf_ref, bd_ref, rwh_ref, rwl_ref, rb_ref, x1_ref, h2_ref, route_ref):
    bd = bd_ref[...]
    y = y_ref[...]
    mean = _head_sum(y, bd) * (1.0 / HEAD)
    dy = y - mean
    var = _head_sum(dy * dy, bd) * (1.0 / HEAD)
    yn = dy * lax.rsqrt(var + GN_EPS) * lnxw_ref[...] + lnxb_ref[...] + bonus_ref[...].astype(f32)
    ya = (yn * g_ref[...].astype(f32)).astype(bf16)

    l0, l1, l2 = l0_ref[...], l1_ref[...], l2_ref[...]
    lm = jnp.maximum(jnp.maximum(l0, l1), l2)
    e0, e1, e2 = jnp.exp(l0 - lm), jnp.exp(l1 - lm), jnp.exp(l2 - lm)
    yb = (e0 * o0_ref[...].astype(f32) + e1 * o1_ref[...].astype(f32)
          + e2 * o2_ref[...].astype(f32)) / (e0 + e1 + e2)

    gates = _sigmoid(zg_ref[...].astype(f32) + gb_ref[...])
    d = D_MODEL
    merged = (gates[:, 0:d] * jnp.dot(ya, wpa_ref[...], preferred_element_type=f32)
              + gates[:, d:2 * d] * _bdot(yb, wpb_ref[...])
              + gates[:, 2 * d:3 * d] * jnp.dot(yc_ref[...], wpc_ref[...], preferred_element_type=f32))
    x1 = x_ref[...] + _bdot(merged, wout_ref[...])
    x1_ref[...] = x1
    hn = x1 * lax.rsqrt(jnp.mean(x1 * x1, axis=-1, keepdims=True) + RMS_EPS) * nf_ref[...]
    h2_ref[...] = hn

    hh = hn.astype(bf16)
    hl = (hn - hh.astype(f32)).astype(bf16)
    logits = (jnp.dot(hh, rwh_ref[...], preferred_element_type=f32)
              + jnp.dot(hl, rwh_ref[...], preferred_element_type=f32)
              + jnp.dot(hh, rwl_ref[...], preferred_element_type=f32)) + rb_ref[...]
    lane_i = lax.broadcasted_iota(i32, logits.shape, 1)
    lane = lane_i.astype(f32)
    lane_grp = ((lane_i - N_GROUPS) // EXPERTS_PER_GROUP).astype(f32)
    neg = jnp.float32(-1e30)
    big = jnp.float32(1e9)
    is_g = lane_i < N_GROUPS
    gl = jnp.where(is_g, logits, neg)
    gmax = jnp.max(gl, axis=-1, keepdims=True)
    gsel = jnp.min(jnp.where(gl == gmax, lane, big), axis=-1, keepdims=True)
    gsum = jnp.sum(jnp.where(is_g, jnp.exp(gl - gmax), 0.0), axis=-1, keepdims=True)
    pg = 1.0 / gsum
    emask = (lane_i >= N_GROUPS) & (lane_i < N_GROUPS + N_EXPERTS) & (lane_grp == gsel)
    el = jnp.where(emask, logits, neg)
    t1 = jnp.max(el, axis=-1, keepdims=True)
    i1 = jnp.min(jnp.where(el == t1, lane, big), axis=-1, keepdims=True)
    el2 = jnp.where(lane == i1, neg, el)
    t2 = jnp.max(el2, axis=-1, keepdims=True)
    i2 = jnp.min(jnp.where(el2 == t2, lane, big), axis=-1, keepdims=True)
    e21 = jnp.exp(t2 - t1)
    wa = pg / (1.0 + e21)
    wb = pg * e21 / (1.0 + e21)
    route = jnp.where(lane_i == 0, i1 - N_GROUPS,
                      jnp.where(lane_i == 1, i2 - N_GROUPS,
                                jnp.where(lane_i == 2, wa, jnp.where(lane_i == 3, wb, 0.0))))
    route_ref[...] = route


def _merge(x, y, g, bonus, os_, ls_, yc, zg, p, tm=256):
    n = x.shape[0]
    d = D_MODEL
    tok = lambda c: pl.BlockSpec((tm, c), lambda i: (i, 0))
    par = lambda r_, c_: pl.BlockSpec((r_, c_), lambda i: (0, 0))
    return pl.pallas_call(
        _merge_kernel,
        out_shape=[jax.ShapeDtypeStruct((n, d), f32), jax.ShapeDtypeStruct((n, d), f32),
                   jax.ShapeDtypeStruct((n, LANES), f32)],
        grid=(n // tm,),
        in_specs=[tok(d), tok(d), tok(d), tok(d), tok(256), tok(256), tok(256), tok(256), tok(256), tok(256),
                  tok(d), tok(G_COLS), par(1, d), par(1, d), par(1, G_COLS), par(d, d), par(256, d),
                  par(d, d), par(d, d), par(1, d), par(LANES, LANES), par(d, LANES), par(d, LANES),
                  par(1, LANES)],
        out_specs=[tok(d), tok(d), tok(LANES)],
        compiler_params=_cparams("parallel"),
        name="merge_router",
    )(x, y, g, bonus, *os_, *ls_, yc, zg, p["lnx_w"], p["lnx_b"], p["gate_b"], p["w_pa"], p["w_pb"],
      p["w_pc"], p["w_out"], p["norm_ffn"], p["bd"], p["rw_hi"], p["rw_lo"], p["rb"])


def _moe_kernel(blk_e_ref, nused_ref, tok_ref, dst_ref, h2_hbm, w1_ref, w3_ref, w2_ref, out_hbm,
                xbuf, ybuf, sem_in, sem_out):
    j = pl.program_id(0)
    rows = MOE_BLOCK

    def gather_copy(i):
        return pltpu.make_async_copy(h2_hbm.at[pl.ds(tok_ref[0, 0, i], 1), :],
                                     xbuf.at[pl.ds(i, 1), :], sem_in)

    def scatter_copy(i):
        return pltpu.make_async_copy(ybuf.at[pl.ds(i, 1), :],
                                     out_hbm.at[pl.ds(dst_ref[0, 0, i], 1), :], sem_out)

    def each(fn):
        def body(i, c):
            fn(i)
            return c
        lax.fori_loop(0, rows, body, 0)

    @pl.when(j < nused_ref[0])
    def _():
        each(lambda i: gather_copy(i).start())
        each(lambda i: gather_copy(i).wait())
        x = xbuf[...].astype(bf16)
        h1 = jnp.dot(x, w1_ref[...], preferred_element_type=f32)
        h3 = jnp.dot(x, w3_ref[...], preferred_element_type=f32)
        hid = (h1 * _sigmoid(h1) * h3).astype(bf16)
        ybuf[...] = jnp.dot(hid, w2_ref[...], preferred_element_type=f32)
        each(lambda i: pl.when(dst_ref[0, 0, i] >= 0)(lambda: scatter_copy(i).start()))
        each(lambda i: pl.when(dst_ref[0, 0, i] >= 0)(lambda: scatter_copy(i).wait()))


def _moe(h2, blk_e, nused, tok, dst, w1, w3, w2, nblk, n_out_rows):
    d = D_MODEL
    grid_spec = pltpu.PrefetchScalarGridSpec(
        num_scalar_prefetch=2,
        grid=(nblk,),
        in_specs=[pl.BlockSpec((1, 1, MOE_BLOCK), lambda j, be, nu: (j, 0, 0), memory_space=pltpu.SMEM),
                  pl.BlockSpec((1, 1, MOE_BLOCK), lambda j, be, nu: (j, 0, 0), memory_space=pltpu.SMEM),
                  pl.BlockSpec(memory_space=pl.ANY),
                  pl.BlockSpec((None, d, EXPERT_FF), lambda j, be, nu: (be[j], 0, 0)),
                  pl.BlockSpec((None, d, EXPERT_FF), lambda j, be, nu: (be[j], 0, 0)),
                  pl.BlockSpec((None, EXPERT_FF, d), lambda j, be, nu: (be[j], 0, 0))],
        out_specs=pl.BlockSpec(memory_space=pl.ANY),
        scratch_shapes=[pltpu.VMEM((MOE_BLOCK, d), f32), pltpu.VMEM((MOE_BLOCK, d), f32),
                        pltpu.SemaphoreType.DMA(()), pltpu.SemaphoreType.DMA(())],
    )
    return pl.pallas_call(
        _moe_kernel,
        out_shape=jax.ShapeDtypeStruct((n_out_rows, d), f32),
        grid_spec=grid_spec,
        compiler_params=_cparams("arbitrary"),
        name="moe_experts",
    )(blk_e, nused, tok, dst, h2, w1, w3, w2)


def _dispatch(route, n):
    nk = 2 * n
    flat_e = route[:, 0:2].astype(i32).reshape(-1)
    order = jnp.argsort(flat_e, stable=True).astype(i32)
    se = flat_e[order]
    counts = jnp.bincount(flat_e, length=N_EXPERTS).astype(i32)
    padded = (counts + MOE_BLOCK - 1) // MOE_BLOCK * MOE_BLOCK
    pend = jnp.cumsum(padded)
    pstart = pend - padded
    cstart = jnp.cumsum(counts) - counts
    dest = pstart[se] + jnp.arange(nk, dtype=i32) - cstart[se]
    nblk = nk // MOE_BLOCK + N_EXPERTS
    slots = nblk * MOE_BLOCK
    tok =jnp.zeros((slots,), i32).at[dest].set(order // 2)
    dst = jnp.full((slots,), -1, i32).at[dest].set((order % 2) * n + order // 2)
    blk_e = jnp.minimum(jnp.searchsorted(pend, jnp.arange(nblk, dtype=i32) * MOE_BLOCK, side="right"),
                        N_EXPERTS - 1).astype(i32)
    nused = (pend[-1] // MOE_BLOCK).astype(i32).reshape(1)
    return blk_e, nused, tok.reshape(nblk, 1, MOE_BLOCK), dst.reshape(nblk, 1, MOE_BLOCK), nblk


def _combine_kernel(emit_x, x1_ref, ya_ref, yb_ref, route_ref, g_ref, *outs):
    route = route_ref[...]
    x2 = x1_ref[...] + route[:, 2:3] * ya_ref[...] + route[:, 3:4] * yb_ref[...]
    hn = x2 * lax.rsqrt(jnp.mean(x2 * x2, axis=-1, keepdims=True) + RMS_EPS) * g_ref[...]
    if emit_x:
        outs[0][...] = x2
        outs[1][...] = hn.astype(outs[1].dtype)
    else:
        outs[0][...] = hn.astype(outs[0].dtype)


def _combine(x1, out2, route, g, emit_x, tm=512):
    n, d = x1.shape
    nt = n // tm
    tok = lambda c: pl.BlockSpec((tm, c), lambda i: (i, 0))
    if emit_x:
        out_shape = [jax.ShapeDtypeStruct((n, d), f32), jax.ShapeDtypeStruct((n, d), bf16)]
        out_specs = [tok(d), tok(d)]
    else:
        out_shape = [jax.ShapeDtypeStruct((n, d), f32)]
        out_specs = [tok(d)]
    return pl.pallas_call(
        functools.partial(_combine_kernel, emit_x),
        out_shape=out_shape,
        grid=(nt,),
        in_specs=[tok(d), tok(d), pl.BlockSpec((tm, d), lambda i: (i + nt, 0)), tok(LANES),
                  pl.BlockSpec((1, d), lambda i: (0, 0))],
        out_specs=out_specs,
        compiler_params=_cparams("parallel"),
        name="moe_combine",
    )(x1, out2, out2, route, g.reshape(1, d))


def _rope_table(T):
    pos = jnp.arange(T, dtype=f32)
    inv_freq = ROPE_THETA ** (-jnp.arange(0, HEAD, 2, dtype=f32) / HEAD)
    ang = pos[:, None] * inv_freq[None, :]
    cos, sin = jnp.cos(ang), jnp.sin(ang)
    cos_h = jnp.concatenate([cos, cos], axis=-1)
    sin_h = jnp.concatenate([-sin, sin], axis=-1)
    return jnp.concatenate([jnp.tile(cos_h, (1, 4)), jnp.tile(sin_h, (1, 4))], axis=-1)


def _pad_rows(w, rows, at=0):
    out = jnp.zeros((rows, w.shape[1]), w.dtype)
    return out.at[at:at + w.shape[0]].set(w)


def kernel(x, norm_mix, w_in, mu_a, w_decay0, w_decay2, a0, a2, g2, k_k, k_a, r_k, lnx_w, lnx_b, vres_mu, vres_w1, vres_w2, vres_b, sgu_ln_w, sgu_ln_b, sgu_ws, sgu_bs, gate_b, w_pa, w_pb, w_pc, w_out, norm_ffn, router_grp, router_grp_b, router_exp, router_exp_b, moe_w1, moe_w3, moe_w2, norm_final):
    bsz, T, d = x.shape
    n = bsz * T
    depth = w_in.shape[0]
    rope_tab = _rope_table(T)
    head_id = jnp.arange(LANES) // HEAD
    bd = (head_id[:, None] == head_id[None, :]).astype(bf16)
    row = lambda v: v.reshape(1, -1).astype(f32)

    xf = x.reshape(n, d)
    h = _rmsnorm(xf, norm_mix[0], bf16)
    v_first = None
    for l in range(depth):
        wl = w_in[l].astype(bf16)
        c0, c1, c2 = A_COLS, A_COLS + B_COLS, A_COLS + B_COLS + C_COLS
        za = _proj(h, wl[:, :c0], A_COLS // 2, "proj_a")
        zb = _proj(h, wl[:, c0:c1], B_COLS // 2, "proj_b")
        zc = _proj(h, wl[:, c1:c2], C_COLS // 2, "proj_c")
        zg = _proj(h, wl[:, c2:], G_COLS // 2, "proj_g")

        pa = dict(mu=row(mu_a[l]), w0=row(w_decay0[l]), w2=_pad_rows(w_decay2[l], LANES, 0).astype(bf16),
                  a0=row(a0[l]), a2=_pad_rows(a2[l], LANES, 64).astype(bf16), g2=g2[l].astype(bf16),
                  k_k=row(k_k[l]), k_a=row(k_a[l]), r_k=row(r_k[l]), bd=bd)
        vres = None
        if l > 0:
            w1p = jnp.zeros((d, LANES), f32).at[:, :vres_w1.shape[2]].set(vres_w1[l - 1]).astype(bf16)
            mup = jnp.zeros((1, LANES), f32).at[0, :vres_mu.shape[1]].set(vres_mu[l - 1])
            vres = dict(h=h, vfirst=v_first, w1=w1p, mu=mup,
                        w2=_pad_rows(vres_w2[l - 1], LANES, 0).astype(bf16), b=row(vres_b[l - 1]))
        rt, at, v, bT, kT, pcT, g, bonus = _wkv_prep(za, bsz, T, pa, vres)
        if l == 0:
            v_first = v
        y = _wkv_scan(rt, at, v, bT, kT, pcT, bsz, T).reshape(n, d)

        os_, ls_ = [], []
        for gi, dil in enumerate(DILATIONS):
            o, lse = _attention_group(zb, rope_tab, bsz, T, gi, dil)
            os_.append(o)
            ls_.append(lse)

        bs_b = jnp.broadcast_to(sgu_bs[l][:, :, None], (SGU_GROUPS, SGU_CHUNK, LANES)).astype(f32)
        yc = _sgu(zc, row(sgu_ln_w[l]), row(sgu_ln_b[l]), sgu_ws[l], bs_b)

        rw = jnp.zeros((d, LANES), f32).at[:, :N_GROUPS].set(router_grp[l]) \
            .at[:, N_GROUPS:N_GROUPS + N_EXPERTS].set(router_exp[l])
        rw_hi = rw.astype(bf16)
        rw_lo = (rw - rw_hi.astype(f32)).astype(bf16)
        rb = jnp.zeros((1, LANES), f32).at[0, :N_GROUPS].set(router_grp_b[l]) \
            .at[0, N_GROUPS:N_GROUPS + N_EXPERTS].set(router_exp_b[l])
        pm = dict(lnx_w=row(lnx_w[l]), lnx_b=row(lnx_b[l]), gate_b=row(gate_b[l]), w_pa=w_pa[l].astype(bf16),
                  w_pb=w_pb[l].astype(bf16), w_pc=w_pc[l].astype(bf16), w_out=w_out[l].astype(bf16),
                  norm_ffn=row(norm_ffn[l]), bd=bd, rw_hi=rw_hi, rw_lo=rw_lo, rb=rb)
        x1, h2, route = _merge(xf, y, g.reshape(n, d), bonus.reshape(n, d), os_, ls_, yc, zg, pm)

        blk_e, nused, tok, dst, nblk = _dispatch(route, n)
        out2 = _moe(h2, blk_e, nused, tok, dst, moe_w1[l].astype(bf16), moe_w3[l].astype(bf16),
                    moe_w2[l].astype(bf16), nblk, 2 * n)
        if l + 1 < depth:
            xf, h = _combine(x1, out2, route, norm_mix[l + 1], True)
        else:
            (out,) = _combine(x1, out2, route, norm_final, False)
    return out.reshape(bsz, T, d)
```

```python
import functools
import math

import jax
import jax.numpy as jnp
from jax import lax
from jax.experimental import pallas as pl
from jax.experimental.pallas import tpu as pltpu

f32 = jnp.float32
bf16 = jnp.bfloat16
i32 = jnp.int32

D_MODEL = 1024
HEAD = 64
RMS_EPS = 1e-6
GN_EPS = 64e-5
LN_EPS = 1e-5
ROPE_THETA = 10000.0
A_WIDTH = D_MODEL
A_COLS = 3 * A_WIDTH + 64 + 64 + 128
B_COLS = 3 * 768
C_COLS = 2 * D_MODEL
G_COLS = 3 * D_MODEL
DILATIONS = (1, 4, 16)
ATT_BLOCK = 128
ATT_SPAN = 128
SGU_CHUNK = 128
SGU_GROUPS = 8
N_GROUPS = 4
EXPERTS_PER_GROUP = 8
N_EXPERTS = N_GROUPS * EXPERTS_PER_GROUP
EXPERT_FF = D_MODEL // 2
MOE_BLOCK = 256
WKV_CHUNK = 128
SCAN_GROUP = 8
LANES = 128
VMEM_LIMIT = 48 * 1024 * 1024


def _cparams(*sem):
    return pltpu.CompilerParams(dimension_semantics=sem, vmem_limit_bytes=VMEM_LIMIT)


def _bdot(a, b):
    return jnp.dot(a.astype(bf16), b.astype(bf16), preferred_element_type=f32)


def _dot_hilo(a, b):
    hi = a.astype(bf16)
    lo = (a - hi.astype(f32)).astype(bf16)
    return (jnp.dot(hi, b, preferred_element_type=f32)
            + jnp.dot(lo, b, preferred_element_type=f32))


def _head_sum(x, bd):
    parts = [_dot_hilo(x[:, p * LANES:(p + 1) * LANES], bd) for p in range(x.shape[1] // LANES)]
    return jnp.concatenate(parts, axis=1)


def _sigmoid(x):
    return 1.0 / (1.0 + jnp.exp(-x))


def _rmsnorm_kernel(x_ref, g_ref, o_ref):
    x = x_ref[...]
    y = x * lax.rsqrt(jnp.mean(x * x, axis=-1, keepdims=True) + RMS_EPS)
    o_ref[...] = (y * g_ref[...]).astype(o_ref.dtype)


def _rmsnorm(x, g, out_dtype, tm=1024):
    n, d = x.shape
    return pl.pallas_call(
        _rmsnorm_kernel,
        out_shape=jax.ShapeDtypeStruct((n, d), out_dtype),
        grid=(n // tm,),
        in_specs=[pl.BlockSpec((tm, d), lambda i: (i, 0)), pl.BlockSpec((1, d), lambda i: (0, 0))],
        out_specs=pl.BlockSpec((tm, d), lambda i: (i, 0)),
        compiler_params=_cparams("parallel"),
        name="rmsnorm",
    )(x, g.reshape(1, d))


def _matmul_kernel(h_ref, w_ref, o_ref):
    o_ref[...] = jnp.dot(h_ref[...], w_ref[...], preferred_element_type=f32).astype(o_ref.dtype)


def _proj(h, w, tn, name, tm=1024):
    n, k = h.shape
    m = w.shape[1]
    return pl.pallas_call(
        _matmul_kernel,
        out_shape=jax.ShapeDtypeStruct((n, m), bf16),
        grid=(m // tn, n // tm),
        in_specs=[pl.BlockSpec((tm, k), lambda j, i: (i, 0)), pl.BlockSpec((k, tn), lambda j, i: (0, j))],
        out_specs=pl.BlockSpec((tm, tn), lambda j, i: (i, j)),
        compiler_params=_cparams("parallel", "parallel"),
        name=name,
    )(h, w)


def _wkv_prep_kernel(has_vres, tm, *refs):
    if has_vres:
        (za_ref, h_ref, vfirst_ref, mu_ref, w0_ref, w2_ref, a0_ref, a2_ref, g2_ref, kk_ref, ka_ref,
         rk_ref, bd_ref, vw1_ref, vmu_ref, vw2_ref, vb_ref,
         rt_ref, at_ref, v_ref, bT_ref, kT_ref, pcT_ref, g_ref, bonus_ref, carry, carry_v) = refs
    else:
        (za_ref, mu_ref, w0_ref, w2_ref, a0_ref, a2_ref, g2_ref, kk_ref, ka_ref, rk_ref, bd_ref,
         rt_ref, at_ref, v_ref, bT_ref, kT_ref, pcT_ref, g_ref, bonus_ref, carry) = refs
    t = pl.program_id(1)

    @pl.when(t == 0)
    def _():
        carry[...] = jnp.zeros_like(carry)
        if has_vres:
            carry_v[...] = jnp.zeros_like(carry_v)

    row = lax.broadcasted_iota(i32, (tm, 1), 0)
    z = za_ref[...].astype(f32)
    prev = jnp.where(row == 0, carry[0:1, :], pltpu.roll(z, 1, axis=0))
    carry[0:1, :] = z[tm - 1:tm, :]
    zs = z + (prev - z) * mu_ref[...]
    w = A_WIDTH
    r, k, v = zs[:, 0:w], zs[:, w:2 * w], zs[:, 2 * w:3 * w]
    lo = zs[:, 3 * w:3 * w + LANES]
    glo = zs[:, 3 * w + LANES:3 * w + 2 * LANES]

    if has_vres:
        hv = jnp.dot(h_ref[...], vw1_ref[...], preferred_element_type=f32)
        prev_v = jnp.where(row == 0, carry_v[0:1, :], pltpu.roll(hv, 1, axis=0))
        carry_v[0:1, :] = hv[tm - 1:tm, :]
        vlo = hv + (prev_v - hv) * vmu_ref[...]
        sg = _sigmoid(vb_ref[...] + _bdot(vlo, vw2_ref[...]))
        v = v + (vfirst_ref[...].astype(f32) - v) * sg

    wl = w0_ref[...] + _bdot(jnp.tanh(lo), w2_ref[...])
    yy = -wl
    softplus = jnp.maximum(yy, 0.0) + jnp.log(1.0 + jnp.exp(-jnp.abs(yy)))
    ld = -jnp.exp(-softplus - 0.5)
    a = _sigmoid(a0_ref[...] + _bdot(lo, a2_ref[...]))
    g = _bdot(_sigmoid(glo), g2_ref[...])
    bd = bd_ref[...]
    kk = k * kk_ref[...]
    kk = kk / jnp.maximum(jnp.sqrt(_head_sum(kk * kk, bd)), 1e-12)
    kp = k * (1.0 + (a - 1.0) * ka_ref[...])
    bonus = _head_sum(r * kp * rk_ref[...], bd) * v

    tloc = row % WKV_CHUNK
    cum = ld
    s = 1
    while s < WKV_CHUNK:
        cum = cum + jnp.where(tloc >= s, pltpu.roll(cum, s, axis=0), 0.0)
        s *= 2
    p_inc = jnp.exp(cum)
    p_inv = jnp.exp(-cum)
    ends = [cum[(c + 1) * WKV_CHUNK - 1:(c + 1) * WKV_CHUNK, :] for c in range(tm // WKV_CHUNK)]
    cum_end = ends[0]
    for c in range(1, tm // WKV_CHUNK):
        cum_end = jnp.where(row >= c * WKV_CHUNK, ends[c], cum_end)
    cum_end = jnp.broadcast_to(cum_end, (tm, w))

    rt_ref[...] = (r * p_inc).astype(bf16)
    at_ref[...] = (-kk * jnp.exp(cum - ld)).astype(bf16)
    v_ref[...] = v.astype(bf16)
    bT_ref[...] = (kk * a * p_inv).T.astype(bf16)
    kT_ref[...] = (kp * p_inv).T.astype(bf16)
    pcT_ref[...] = jnp.exp(cum_end).T
    g_ref[...] = g.astype(bf16)
    bonus_ref[...] = bonus.astype(bf16)


def _wkv_prep(za, bsz, T, p, vres, tm=256):
    has_vres = vres is not None
    w = A_WIDTH
    tok = lambda cols: pl.BlockSpec((None, tm, cols), lambda b, t: (b, t, 0))
    par = lambda r_, c_: pl.BlockSpec((r_, c_), lambda b, t: (0, 0))
    ins = [za.reshape(bsz, T, A_COLS)]
    specs = [tok(A_COLS)]
    if has_vres:
        ins += [vres["h"].reshape(bsz, T, D_MODEL), vres["vfirst"]]
        specs += [tok(D_MODEL), tok(w)]
    ins += [p["mu"], p["w0"], p["w2"], p["a0"], p["a2"], p["g2"], p["k_k"], p["k_a"], p["r_k"], p["bd"]]
    specs += [par(1, A_COLS), par(1, w), par(LANES, w), par(1, w), par(LANES, w), par(LANES, w),
              par(1, w), par(1, w), par(1, w), par(LANES, LANES)]
    if has_vres:
        ins += [vres["w1"], vres["mu"], vres["w2"], vres["b"]]
        specs += [par(D_MODEL, LANES), par(1, LANES), par(LANES, w), par(1, w)]
    tr = pl.BlockSpec((None, w, tm), lambda b, t: (b, 0, t))
    out_shape = [jax.ShapeDtypeStruct((bsz, T, w), bf16)] * 3 + \
                [jax.ShapeDtypeStruct((bsz, w, T), bf16)] * 2 + \
                [jax.ShapeDtypeStruct((bsz, w, T), f32)] + \
                [jax.ShapeDtypeStruct((bsz, T, w), bf16)] * 2
    out_specs = [tok(w)] * 3 + [tr] * 3 + [tok(w)] * 2
    scratch = [pltpu.VMEM((8, A_COLS), f32)]
    if has_vres:
        scratch.append(pltpu.VMEM((8, LANES), f32))
    return pl.pallas_call(
        functools.partial(_wkv_prep_kernel, has_vres, tm),
        out_shape=out_shape,
        grid=(bsz, T // tm),
        in_specs=specs,
        out_specs=out_specs,
        scratch_shapes=scratch,
        compiler_params=_cparams("parallel", "arbitrary"),
        name="wkv_prep",
    )(*ins)


def _wkv_scan_kernel(rt_ref, at_ref, v_ref, bT_ref, kT_ref, pcT_ref, y_ref, st_ref):
    C = WKV_CHUNK

    @pl.when(pl.program_id(1) == 0)
    def _():
        st_ref[...] = jnp.zeros_like(st_ref)

    head0 = lax.broadcasted_iota(i32, (1, LANES), 1) < HEAD
    key0 = lax.broadcasted_iota(i32, (LANES, 1), 0) < HEAD
    same_head = head0 == key0
    ri = lax.broadcasted_iota(i32, (C, C), 0)
    ci = lax.broadcasted_iota(i32, (C, C), 1)
    strict = ri > ci
    lower2 = jnp.concatenate([ri >= ci, ri >= ci], axis=1)
    eye = (ri == ci).astype(f32)

    fdot = lambda a, b: jnp.dot(a, b, preferred_element_type=f32)
    hms = (head0, jnp.logical_not(head0))

    def pair_group(gi, carry_):
        pis = [gi * SCAN_GROUP + q for q in range(SCAN_GROUP)]
        c0s = [pl.multiple_of(pi * LANES, LANES) for pi in pis]
        ars = [jnp.concatenate([at_ref[:, pl.ds(c0, LANES)], rt_ref[:, pl.ds(c0, LANES)]], axis=0) for c0 in c0s]
        vs = [v_ref[:, pl.ds(c0, LANES)] for c0 in c0s]
        bks = [jnp.concatenate([bT_ref[pl.ds(c0, LANES), :], kT_ref[pl.ds(c0, LANES), :]], axis=1) for c0 in c0s]
        sts = [st_ref[pi] for pi in pis]
        xss = [fdot(ar, st.astype(bf16)) for ar, st in zip(ars, sts)]
        heads = [(q, hm) for q in range(SCAN_GROUP) for hm in hms]
        grams = [fdot(jnp.where(hm, ars[q], 0), bks[q]) for q, hm in heads]
        npows = [jnp.where(strict, gm[0:C, 0:C], 0.0) for gm in grams]
        a_aks = [jnp.where(strict, gm[0:C, C:2 * C], 0.0).astype(bf16) for gm in grams]
        a_rs = [jnp.where(lower2, gm[C:2 * C, :], 0.0).astype(bf16) for gm in grams]
        xins = [xss[q][0:C] + fdot(a_ak, vs[q]) for (q, _), a_ak in zip(heads, a_aks)]
        winvs = [eye + n for n in npows]
        s = 2
        while s < C:
            npows = [_bdot(n, n) for n in npows]
            winvs = [w + _bdot(w, n) for w, n in zip(winvs, npows)]
            s *= 2
        us = [_bdot(w, x) for w, x in zip(winvs, xins)]
        for q in range(SCAN_GROUP):
            uv = jnp.concatenate([jnp.where(head0, us[2 * q], us[2 * q + 1]).astype(bf16), vs[q]], axis=0)
            y_ref[:, pl.ds(c0s[q], LANES)] = xss[q][C:2 * C] + jnp.where(
                head0, fdot(a_rs[2 * q], uv), fdot(a_rs[2 * q + 1], uv))
            pc = pcT_ref[pl.ds(c0s[q], LANES), :]
            bkp = (bks[q].astype(f32) * jnp.concatenate([pc, pc], axis=1)).astype(bf16)
            st_ref[pis[q]] = sts[q] * pc + jnp.where(same_head, fdot(bkp, uv), 0.0)
        return carry_

    lax.fori_loop(0, A_WIDTH // LANES // SCAN_GROUP, pair_group, 0)


def _wkv_scan(rt, at, v, bT, kT, pcT, bsz, T):
    C = WKV_CHUNK
    w = A_WIDTH
    tok = pl.BlockSpec((None, C, w), lambda b, t: (b, t, 0))
    tr = pl.BlockSpec((None, w, C), lambda b, t: (b, 0, t))
    return pl.pallas_call(
        _wkv_scan_kernel,
        out_shape=jax.ShapeDtypeStruct((bsz, T, w), f32),
        grid=(bsz, T // C),
        in_specs=[tok, tok, tok, tr, tr, tr],
        out_specs=tok,
        scratch_shapes=[pltpu.VMEM((w // LANES, LANES, LANES), f32)],
        compiler_params=_cparams("parallel", "arbitrary"),
        name="wkv_scan",
    )(rt, at, v, bT, kT, pcT)


def _attn_kernel(q_ref, kc_ref, kp_ref, vc_ref, vp_ref, cq_ref, sq_ref, cp_ref, sp_ref, o_ref, lse_ref):
    n = pl.program_id(2)
    blk = ATT_BLOCK
    width = q_ref.shape[-1]
    lane = lax.broadcasted_iota(i32, (1, width), 1)
    first_half = (lane % HEAD) < (HEAD // 2)

    def rope(z_ref, c_ref, s_ref):
        z = z_ref[...].astype(f32)
        partner = jnp.where(first_half, pltpu.roll(z, width - HEAD // 2, axis=1),
                            pltpu.roll(z, HEAD // 2, axis=1))
        return z * c_ref[...] + partner * s_ref[...]

    q = rope(q_ref, cq_ref, sq_ref) * (HEAD ** -0.5)
    kcat = jnp.concatenate([rope(kp_ref, cp_ref, sp_ref), rope(kc_ref, cq_ref, sq_ref)], axis=0).astype(bf16)
    vcat = jnp.concatenate([vp_ref[...], vc_ref[...]], axis=0)
    qi = lax.broadcasted_iota(i32, (blk, 2 * blk), 0)
    kj = lax.broadcasted_iota(i32, (blk, 2 * blk), 1)
    dist = qi + blk - kj
    mask = (dist >= 0) & (dist <= ATT_SPAN) & ((kj >= blk) | (n > 0))
    o_acc = jnp.zeros((blk, width), f32)
    lse_acc = jnp.zeros((blk, width), f32)
    for h in range(width // HEAD):
        hm = (lane // HEAD) == h
        qh = jnp.where(hm, q, 0.0).astype(bf16)
        s = lax.dot_general(qh, kcat, (((1,), (1,)), ((), ())), preferred_element_type=f32)
        s = jnp.where(mask, s, -jnp.inf)
        m = jnp.max(s, axis=-1, keepdims=True)
        e = jnp.exp(s - m)
        den = jnp.sum(e, axis=-1, keepdims=True)
        oh = jnp.dot(e.astype(bf16), vcat, preferred_element_type=f32) / den
        o_acc = jnp.where(hm, oh, o_acc)
        lse_acc = jnp.where(hm, m + jnp.log(den), lse_acc)
    o_ref[...] = o_acc.astype(o_ref.dtype)
    lse_ref[...] = lse_acc


def _attention_group(zb, rope_tab, bsz, T, gi, dil):
    blk = ATT_BLOCK
    L = T // dil
    nb = L // blk
    width = 4 * HEAD
    cb = B_COLS // width
    zv = zb.reshape(bsz, L, dil * B_COLS)
    tab = rope_tab.reshape(L, dil * 2 * width)
    cur = lambda off: pl.BlockSpec((None, blk, width), lambda b, r, n: (b, n, r * cb + off + gi))
    prv = lambda off: pl.BlockSpec((None, blk, width), lambda b, r, n: (b, jnp.maximum(n - 1, 0), r * cb + off + gi))
    tcur = lambda off: pl.BlockSpec((blk, width), lambda b, r, n: (n, 2 * r + off))
    tprv = lambda off: pl.BlockSpec((blk, width), lambda b, r, n: (jnp.maximum(n - 1, 0), 2 * r + off))
    ospec = pl.BlockSpec((None, blk, width), lambda b, r, n: (b, n, r))
    o, lse = pl.pallas_call(
        _attn_kernel,
        out_shape=[jax.ShapeDtypeStruct((bsz, L, dil * width), bf16),
                   jax.ShapeDtypeStruct((bsz, L, dil * width), f32)],
        grid=(bsz, dil, nb),
        in_specs=[cur(0), cur(3), prv(3), cur(6), prv(6), tcur(0), tcur(1), tprv(0), tprv(1)],
        out_specs=[ospec, ospec],
        compiler_params=_cparams("parallel", "parallel", "arbitrary"),
        name=f"dilated_attn_d{dil}",
    )(zv, zv, zv, zv, zv, tab, tab, tab, tab)
    return o.reshape(bsz * T, width), lse.reshape(bsz * T, width)


def _sgu_kernel(tm, zc_ref, lnw_ref, lnb_ref, ws_ref, bs_ref, o_ref):
    ch = SGU_CHUNK
    z = zc_ref[...].astype(f32)
    ge = 0.5 * z * (1.0 + lax.erf(z * (2.0 ** -0.5)))
    u = ge[:, :D_MODEL]
    v = ge[:, D_MODEL:]
    mean = jnp.mean(v, axis=-1, keepdims=True)
    var = jnp.mean(jnp.square(v - mean), axis=-1, keepdims=True)
    vn = ((v - mean) * lax.rsqrt(var + LN_EPS) * lnw_ref[...] + lnb_ref[...]).astype(bf16)
    ri = lax.broadcasted_iota(i32, (ch, ch), 0)
    ci = lax.broadcasted_iota(i32, (ch, ch), 1)
    causal = ri >= ci
    for g in range(SGU_GROUPS):
        wg = jnp.where(causal, ws_ref[g], 0.0).astype(bf16)
        bias = bs_ref[g]
        for c in range(tm // ch):
            rows = slice(c * ch, (c + 1) * ch)
            cols = slice(g * LANES, (g + 1) * LANES)
            mixed = jnp.dot(wg, vn[rows, cols], preferred_element_type=f32) + bias
            o_ref[rows, cols] = (u[rows, cols] * mixed).astype(o_ref.dtype)


def _sgu(zc, lnw, lnb, ws, bs_b, tm=512):
    n = zc.shape[0]
    return pl.pallas_call(
        functools.partial(_sgu_kernel, tm),
        out_shape=jax.ShapeDtypeStruct((n, D_MODEL), bf16),
        grid=(n // tm,),
        in_specs=[pl.BlockSpec((tm, C_COLS), lambda i: (i, 0)),
                  pl.BlockSpec((1, D_MODEL), lambda i: (0, 0)),
                  pl.BlockSpec((1, D_MODEL), lambda i: (0, 0)),
                  pl.BlockSpec((SGU_GROUPS, SGU_CHUNK, SGU_CHUNK), lambda i: (0, 0, 0)),
                  pl.BlockSpec((SGU_GROUPS, SGU_CHUNK, LANES), lambda i: (0, 0, 0))],
        out_specs=pl.BlockSpec((tm, D_MODEL), lambda i: (i, 0)),
        compiler_params=_cparams("parallel"),
        name="sgu",
    )(zc, lnw, lnb, ws, bs_b)


def _merge_kernel(x_ref, y_ref, g_ref, bonus_ref, o0_ref, o1_ref, o2_ref, l0_ref, l1_ref, l2_ref,
                  yc_ref, zg_ref, lnxw_ref, lnxb_ref, gb_ref, wpa_ref, wpb_ref, wpc_ref, wout_ref,
                  nf_ref, bd_ref, rwh_ref, rwl_ref, rb_ref, x1_ref, h2_ref, route_ref):
    bd = bd_ref[...]
    y = y_ref[...]
    mean = _head_sum(y, bd) * (1.0 / HEAD)
    dy = y - mean
    var = _head_sum(dy * dy, bd) * (1.0 / HEAD)
    yn = dy * lax.rsqrt(var + GN_EPS) * lnxw_ref[...] + lnxb_ref[...] + bonus_ref[...].astype(f32)
    ya = (yn * g_ref[...].astype(f32)).astype(bf16)

    l0, l1, l2 = l0_ref[...], l1_ref[...], l2_ref[...]
    lm = jnp.maximum(jnp.maximum(l0, l1), l2)
    e0, e1, e2 = jnp.exp(l0 - lm), jnp.exp(l1 - lm), jnp.exp(l2 - lm)
    yb = (e0 * o0_ref[...].astype(f32) + e1 * o1_ref[...].astype(f32)
          + e2 * o2_ref[...].astype(f32)) / (e0 + e1 + e2)

    gates = _sigmoid(zg_ref[...].astype(f32) + gb_ref[...])
    d = D_MODEL
    merged = (gates[:, 0:d] * jnp.dot(ya, wpa_ref[...], preferred_element_type=f32)
              + gates[:, d:2 * d] * _bdot(yb, wpb_ref[...])
              + gates[:, 2 * d:3 * d] * jnp.dot(yc_ref[...], wpc_ref[...], preferred_element_type=f32))
    x1 = x_ref[...] + _bdot(merged, wout_ref[...])
    x1_ref[...] = x1
    hn = x1 * lax.rsqrt(jnp.mean(x1 * x1, axis=-1, keepdims=True) + RMS_EPS) * nf_ref[...]
    h2_ref[...] = hn

    hh = hn.astype(bf16)
    hl = (hn - hh.astype(f32)).astype(bf16)
    logits = (jnp.dot(hh, rwh_ref[...], preferred_element_type=f32)
              + jnp.dot(hl, rwh_ref[...], preferred_element_type=f32)
              + jnp.dot(hh, rwl_ref[...], preferred_element_type=f32)) + rb_ref[...]
    lane_i = lax.broadcasted_iota(i32, logits.shape, 1)
    lane = lane_i.astype(f32)
    lane_grp = ((lane_i - N_GROUPS) // EXPERTS_PER_GROUP).astype(f32)
    neg = jnp.float32(-1e30)
    big = jnp.float32(1e9)
    is_g = lane_i < N_GROUPS
    gl = jnp.where(is_g, logits, neg)
    gmax = jnp.max(gl, axis=-1, keepdims=True)
    gsel = jnp.min(jnp.where(gl == gmax, lane, big), axis=-1, keepdims=True)
    gsum = jnp.sum(jnp.where(is_g, jnp.exp(gl - gmax), 0.0), axis=-1, keepdims=True)
    pg = 1.0 / gsum
    emask = (lane_i >= N_GROUPS) & (lane_i < N_GROUPS + N_EXPERTS) & (lane_grp == gsel)
    el = jnp.where(emask, logits, neg)
    t1 = jnp.max(el, axis=-1, keepdims=True)
    i1 = jnp.min(jnp.where(el == t1, lane, big), axis=-1, keepdims=True)
    el2 = jnp.where(lane == i1, neg, el)
    t2 = jnp.max(el2, axis=-1, keepdims=True)
    i2 = jnp.min(jnp.where(el2 == t2, lane, big), axis=-1, keepdims=True)
    e21 = jnp.exp(t2 - t1)
    wa = pg / (1.0 + e21)
    wb = pg * e21 / (1.0 + e21)
    route = jnp.where(lane_i == 0, i1 - N_GROUPS,
                      jnp.where(lane_i == 1, i2 - N_GROUPS,
                                jnp.where(lane_i == 2, wa, jnp.where(lane_i == 3, wb, 0.0))))
    route_ref[...] = route


def _merge(x, y, g, bonus, os_, ls_, yc, zg, p, tm=256):
    n = x.shape[0]
    d = D_MODEL
    tok = lambda c: pl.BlockSpec((tm, c), lambda i: (i, 0))
    par = lambda r_, c_: pl.BlockSpec((r_, c_), lambda i: (0, 0))
    return pl.pallas_call(
        _merge_kernel,
        out_shape=[jax.ShapeDtypeStruct((n, d), f32), jax.ShapeDtypeStruct((n, d), f32),
                   jax.ShapeDtypeStruct((n, LANES), f32)],
        grid=(n // tm,),
        in_specs=[tok(d), tok(d), tok(d), tok(d), tok(256), tok(256), tok(256), tok(256), tok(256), tok(256),
                  tok(d), tok(G_COLS), par(1, d), par(1, d), par(1, G_COLS), par(d, d), par(256, d),
                  par(d, d), par(d, d), par(1, d), par(LANES, LANES), par(d, LANES), par(d, LANES),
                  par(1, LANES)],
        out_specs=[tok(d), tok(d), tok(LANES)],
        compiler_params=_cparams("parallel"),
        name="merge_router",
    )(x, y, g, bonus, *os_, *ls_, yc, zg, p["lnx_w"], p["lnx_b"], p["gate_b"], p["w_pa"], p["w_pb"],
      p["w_pc"], p["w_out"], p["norm_ffn"], p["bd"], p["rw_hi"], p["rw_lo"], p["rb"])


def _moe_kernel(blk_e_ref, nv_ref, nused_ref, tok_ref, tok_next_ref, dst_ref, h2_hbm, w1_ref, w3_ref, w2_ref,
                out_hbm, xbuf, ybuf, sem_in, sem_out):
    j = pl.program_id(0)
    rows = MOE_BLOCK
    nused = nused_ref[0]
    slot = j % 2

    def start_gather(idx_ref, buf):
        def body(i, c):
            pltpu.make_async_copy(h2_hbm.at[pl.ds(idx_ref[0, 0, i], 1), :],
                                  xbuf.at[buf, pl.ds(i, 1), :], sem_in.at[buf]).start()
            return c
        lax.fori_loop(0, rows, body, 0, unroll=8)

    def wait_scatter(count):
        bulk = pl.multiple_of((count // 8) * 8, 8)

        @pl.when(bulk > 0)
        def _():
            pltpu.make_async_copy(ybuf.at[pl.ds(0, bulk), :], out_hbm.at[pl.ds(0, bulk), :], sem_out).wait()

        def body(i, c):
            pltpu.make_async_copy(ybuf.at[pl.ds(0, 1), :], out_hbm.at[pl.ds(0, 1), :], sem_out).wait()
            return c
        lax.fori_loop(0, count - bulk, body, 0)

    @pl.when(j == 0)
    def _():
        start_gather(tok_ref, 0)

    @pl.when(j + 1 < nused)
    def _():
        start_gather(tok_next_ref, 1 - slot)

    @pl.when(j < nused)
    def _():
        pltpu.make_async_copy(h2_hbm.at[pl.ds(0, rows), :], xbuf.at[slot], sem_in.at[slot]).wait()
        x = xbuf[slot].astype(bf16)
        h1 = jnp.dot(x, w1_ref[...], preferred_element_type=f32)
        h3 = jnp.dot(x, w3_ref[...], preferred_element_type=f32)
        hid = (h1 * _sigmoid(h1) * h3).astype(bf16)
        y = jnp.dot(hid, w2_ref[...], preferred_element_type=f32)

        @pl.when(j > 0)
        def _():
            wait_scatter(nv_ref[j - 1])

        ybuf[...] = y
        nv = nv_ref[j]

        def body(i, c):
            pltpu.make_async_copy(ybuf.at[pl.ds(i, 1), :],
                                  out_hbm.at[pl.ds(dst_ref[0, 0, i], 1), :], sem_out).start()
            return c

        @pl.when(nv == rows)
        def _():
            lax.fori_loop(0, rows, body, 0, unroll=8)

        @pl.when(nv < rows)
        def _():
            lax.fori_loop(0, nv, body, 0)

        @pl.when(j == nused - 1)
        def _():
            wait_scatter(nv)


def _moe(h2, blk_e, nv, nused, tok, dst, w1, w3, w2, nblk, n_out_rows):
    d = D_MODEL
    idx_spec = lambda f: pl.BlockSpec((1, 1, MOE_BLOCK), f, memory_space=pltpu.SMEM)
    wspec = lambda r_, c_: pl.BlockSpec((None, r_, c_), lambda j, be, nv_, nu: (be[j], 0, 0))
    grid_spec = pltpu.PrefetchScalarGridSpec(
        num_scalar_prefetch=3,
        grid=(nblk,),
        in_specs=[idx_spec(lambda j, be, nv_, nu: (j, 0, 0)),
                  idx_spec(lambda j, be, nv_, nu: (jnp.minimum(j + 1, nblk - 1), 0, 0)),
                  idx_spec(lambda j, be, nv_, nu: (j, 0, 0)),
                  pl.BlockSpec(memory_space=pl.ANY),
                  wspec(d, EXPERT_FF), wspec(d, EXPERT_FF), wspec(EXPERT_FF, d)],
        out_specs=pl.BlockSpec(memory_space=pl.ANY),
        scratch_shapes=[pltpu.VMEM((2, MOE_BLOCK, d), f32), pltpu.VMEM((MOE_BLOCK, d), f32),
                        pltpu.SemaphoreType.DMA((2,)), pltpu.SemaphoreType.DMA(())],
    )
    return pl.pallas_call(
        _moe_kernel,
        out_shape=jax.ShapeDtypeStruct((n_out_rows, d), f32),
        grid_spec=grid_spec,
        compiler_params=_cparams("arbitrary"),
        name="moe_experts",
    )(blk_e, nv, nused, tok, tok, dst, h2, w1, w3, w2)


def _dispatch(route, n):
    nk = 2 * n
    flat_e = route[:, 0:2].astype(i32).reshape(-1)
    order = jnp.argsort(flat_e, stable=True).astype(i32)
    counts = jnp.sum(flat_e[:, None] == jnp.arange(N_EXPERTS, dtype=i32)[None, :], axis=0, dtype=i32)
    padded = (counts + MOE_BLOCK - 1) // MOE_BLOCK * MOE_BLOCK
    pend = jnp.cumsum(padded)
    pstart = pend - padded
    cstart = jnp.cumsum(counts) - counts
    nblk = nk // MOE_BLOCK + N_EXPERTS
    blk_start = jnp.arange(nblk, dtype=i32) * MOE_BLOCK
    blk_e = jnp.minimum(jnp.sum(pend[None, :] <= blk_start[:, None], axis=1, dtype=i32), N_EXPERTS - 1)
    off = blk_start - pstart[blk_e]
    nv = jnp.clip(counts[blk_e] - off, 0, MOE_BLOCK).astype(i32)
    within = jnp.arange(MOE_BLOCK, dtype=i32)[None, :]
    src = jnp.clip((cstart[blk_e] + off)[:, None] + within, 0, nk - 1)
    assign = order[src]
    valid = within < nv[:, None]
    tok = jnp.where(valid, assign // 2, 0)
    dst = jnp.where(valid, (assign % 2) * n + assign // 2, 0)
    nused = (pend[-1] // MOE_BLOCK).astype(i32).reshape(1)
    return (blk_e, nv, nused, tok.reshape(nblk, 1, MOE_BLOCK), dst.reshape(nblk, 1, MOE_BLOCK), nblk)


def _combine_kernel(emit_x, x1_ref, ya_ref, yb_ref, route_ref, g_ref, *outs):
    route = route_ref[...]
    x2 = x1_ref[...] + route[:, 2:3] * ya_ref[...] + route[:, 3:4] * yb_ref[...]
    hn = x2 * lax.rsqrt(jnp.mean(x2 * x2, axis=-1, keepdims=True) + RMS_EPS) * g_ref[...]
    if emit_x:
        outs[0][...] = x2
        outs[1][...] = hn.astype(outs[1].dtype)
    else:
        outs[0][...] = hn.astype(outs[0].dtype)


def _combine(x1, out2, route, g, emit_x, tm=512):
    n, d = x1.shape
    nt = n // tm
    tok = lambda c: pl.BlockSpec((tm, c), lambda i: (i, 0))
    if emit_x:
        out_shape = [jax.ShapeDtypeStruct((n, d), f32), jax.ShapeDtypeStruct((n, d), bf16)]
        out_specs = [tok(d), tok(d)]
    else:
        out_shape = [jax.ShapeDtypeStruct((n, d), f32)]
        out_specs = [tok(d)]
    return pl.pallas_call(
        functools.partial(_combine_kernel, emit_x),
        out_shape=out_shape,
        grid=(nt,),
        in_specs=[tok(d), tok(d), pl.BlockSpec((tm, d), lambda i: (i + nt, 0)), tok(LANES),
                  pl.BlockSpec((1, d), lambda i: (0, 0))],
        out_specs=out_specs,
        compiler_params=_cparams("parallel"),
        name="moe_combine",
    )(x1, out2, out2, route, g.reshape(1, d))


def _rope_table(T):
    pos = jnp.arange(T, dtype=f32)
    inv_freq = ROPE_THETA ** (-jnp.arange(0, HEAD, 2, dtype=f32) / HEAD)
    ang = pos[:, None] * inv_freq[None, :]
    cos, sin = jnp.cos(ang), jnp.sin(ang)
    cos_h = jnp.concatenate([cos, cos], axis=-1)
    sin_h = jnp.concatenate([-sin, sin], axis=-1)
    return jnp.concatenate([jnp.tile(cos_h, (1, 4)), jnp.tile(sin_h, (1, 4))], axis=-1)


def _pad_rows(w, rows, at=0):
    out = jnp.zeros((rows, w.shape[1]), w.dtype)
    return out.at[at:at + w.shape[0]].set(w)


def kernel(x, norm_mix, w_in, mu_a, w_decay0, w_decay2, a0, a2, g2, k_k, k_a, r_k, lnx_w, lnx_b, vres_mu, vres_w1, vres_w2, vres_b, sgu_ln_w, sgu_ln_b, sgu_ws, sgu_bs, gate_b, w_pa, w_pb, w_pc, w_out, norm_ffn, router_grp, router_grp_b, router_exp, router_exp_b, moe_w1, moe_w3, moe_w2, norm_final):
    bsz, T, d = x.shape
    n = bsz * T
    depth = w_in.shape[0]
    rope_tab = _rope_table(T)
    head_id = jnp.arange(LANES) // HEAD
    bd = (head_id[:, None] == head_id[None, :]).astype(bf16)
    row = lambda v: v.reshape(1, -1).astype(f32)

    xf = x.reshape(n, d)
    h = _rmsnorm(xf, norm_mix[0], bf16)
    v_first = None
    for l in range(depth):
        wl = w_in[l].astype(bf16)
        c0, c1, c2 = A_COLS, A_COLS + B_COLS, A_COLS + B_COLS + C_COLS
        za = _proj(h, wl[:, :c0], A_COLS // 2, "proj_a")
        zb = _proj(h, wl[:, c0:c1], B_COLS // 2, "proj_b")
        zc = _proj(h, wl[:, c1:c2], C_COLS // 2, "proj_c")
        zg = _proj(h, wl[:, c2:], G_COLS // 2, "proj_g")

        pa = dict(mu=row(mu_a[l]), w0=row(w_decay0[l]), w2=_pad_rows(w_decay2[l], LANES, 0).astype(bf16),
                  a0=row(a0[l]), a2=_pad_rows(a2[l], LANES, 64).astype(bf16), g2=g2[l].astype(bf16),
                  k_k=row(k_k[l]), k_a=row(k_a[l]), r_k=row(r_k[l]), bd=bd)
        vres = None
        if l > 0:
            w1p = jnp.zeros((d, LANES), f32).at[:, :vres_w1.shape[2]].set(vres_w1[l - 1]).astype(bf16)
            mup = jnp.zeros((1, LANES), f32).at[0, :vres_mu.shape[1]].set(vres_mu[l - 1])
            vres = dict(h=h, vfirst=v_first, w1=w1p, mu=mup,
                        w2=_pad_rows(vres_w2[l - 1], LANES, 0).astype(bf16), b=row(vres_b[l - 1]))
        rt, at, v, bT, kT, pcT, g, bonus = _wkv_prep(za, bsz, T, pa, vres)
        if l == 0:
            v_first = v
        y = _wkv_scan(rt, at, v, bT, kT, pcT, bsz, T).reshape(n, d)

        os_, ls_ = [], []
        for gi, dil in enumerate(DILATIONS):
            o, lse = _attention_group(zb, rope_tab, bsz, T, gi, dil)
            os_.append(o)
            ls_.append(lse)

        bs_b = jnp.broadcast_to(sgu_bs[l][:, :, None], (SGU_GROUPS, SGU_CHUNK, LANES)).astype(f32)
        yc = _sgu(zc, row(sgu_ln_w[l]), row(sgu_ln_b[l]), sgu_ws[l], bs_b)

        rw = jnp.zeros((d, LANES), f32).at[:, :N_GROUPS].set(router_grp[l]) \
            .at[:, N_GROUPS:N_GROUPS + N_EXPERTS].set(router_exp[l])
        rw_hi = rw.astype(bf16)
        rw_lo = (rw - rw_hi.astype(f32)).astype(bf16)
        rb = jnp.zeros((1, LANES), f32).at[0, :N_GROUPS].set(router_grp_b[l]) \
            .at[0, N_GROUPS:N_GROUPS + N_EXPERTS].set(router_exp_b[l])
        pm = dict(lnx_w=row(lnx_w[l]), lnx_b=row(lnx_b[l]), gate_b=row(gate_b[l]), w_pa=w_pa[l].astype(bf16),
                  w_pb=w_pb[l].astype(bf16), w_pc=w_pc[l].astype(bf16), w_out=w_out[l].astype(bf16),
                  norm_ffn=row(norm_ffn[l]), bd=bd, rw_hi=rw_hi, rw_lo=rw_lo, rb=rb)
        x1, h2, route = _merge(xf, y, g.reshape(n, d), bonus.reshape(n, d), os_, ls_, yc, zg, pm)

        blk_e, nv, nused, tok, dst, nblk = _dispatch(route, n)
        out2 = _moe(h2, blk_e, nv, nused, tok, dst, moe_w1[l].astype(bf16), moe_w3[l].astype(bf16),
                    moe_w2[l].astype(bf16), nblk, 2 * n)
        if l + 1 < depth:
            xf, h = _combine(x1, out2, route, norm_mix[l + 1], True)
        else:
            (out,) = _combine(x1, out2, route, norm_final, False)
    return out.reshape(bsz, T, d)
```

```python
import functools
import math

import jax
import jax.numpy as jnp
from jax import lax
from jax.experimental import pallas as pl
from jax.experimental.pallas import tpu as pltpu

f32 = jnp.float32
bf16 = jnp.bfloat16
i32 = jnp.int32

D_MODEL = 1024
HEAD = 64
RMS_EPS = 1e-6
GN_EPS = 64e-5
LN_EPS = 1e-5
ROPE_THETA = 10000.0
A_WIDTH = D_MODEL
A_COLS = 3 * A_WIDTH + 64 + 64 + 128
B_COLS = 3 * 768
C_COLS = 2 * D_MODEL
G_COLS = 3 * D_MODEL
DILATIONS = (1, 4, 16)
ATT_BLOCK = 128
ATT_SPAN = 128
ATT_TILE = ATT_BLOCK * max(DILATIONS)
SGU_CHUNK = 128
SGU_GROUPS = 8
N_GROUPS = 4
EXPERTS_PER_GROUP = 8
N_EXPERTS = N_GROUPS * EXPERTS_PER_GROUP
EXPERT_FF = D_MODEL // 2
MOE_BLOCK = 256
WKV_CHUNK = 128
SCAN_GROUP = 8
LANES = 128
VMEM_LIMIT = 48 * 1024 * 1024


def _cparams(*sem):
    return pltpu.CompilerParams(dimension_semantics=sem, vmem_limit_bytes=VMEM_LIMIT)


def _bdot(a, b):
    return jnp.dot(a.astype(bf16), b.astype(bf16), preferred_element_type=f32)


def _dot_hilo(a, b):
    hi = a.astype(bf16)
    lo = (a - hi.astype(f32)).astype(bf16)
    return (jnp.dot(hi, b, preferred_element_type=f32)
            + jnp.dot(lo, b, preferred_element_type=f32))


def _head_sum(x, bd):
    parts = [_dot_hilo(x[:, p * LANES:(p + 1) * LANES], bd) for p in range(x.shape[1] // LANES)]
    return jnp.concatenate(parts, axis=1)


def _sigmoid(x):
    return 1.0 / (1.0 + jnp.exp(-x))


def _rmsnorm_kernel(x_ref, g_ref, o_ref):
    x = x_ref[...]
    y = x * lax.rsqrt(jnp.mean(x * x, axis=-1, keepdims=True) + RMS_EPS)
    o_ref[...] = (y * g_ref[...]).astype(o_ref.dtype)


def _rmsnorm(x, g, out_dtype, tm=1024):
    n, d = x.shape
    return pl.pallas_call(
        _rmsnorm_kernel,
        out_shape=jax.ShapeDtypeStruct((n, d), out_dtype),
        grid=(n // tm,),
        in_specs=[pl.BlockSpec((tm, d), lambda i: (i, 0)), pl.BlockSpec((1, d), lambda i: (0, 0))],
        out_specs=pl.BlockSpec((tm, d), lambda i: (i, 0)),
        compiler_params=_cparams("parallel"),
        name="rmsnorm",
    )(x, g.reshape(1, d))


def _matmul_kernel(h_ref, w_ref, o_ref):
    o_ref[...] = jnp.dot(h_ref[...], w_ref[...], preferred_element_type=f32).astype(o_ref.dtype)


def _proj(h, w, tn, name, tm=1024):
    n, k = h.shape
    m = w.shape[1]
    return pl.pallas_call(
        _matmul_kernel,
        out_shape=jax.ShapeDtypeStruct((n, m), bf16),
        grid=(m // tn, n // tm),
        in_specs=[pl.BlockSpec((tm, k), lambda j, i: (i, 0)), pl.BlockSpec((k, tn), lambda j, i: (0, j))],
        out_specs=pl.BlockSpec((tm, tn), lambda j, i: (i, j)),
        compiler_params=_cparams("parallel", "parallel"),
        name=name,
    )(h, w)


def _wkv_prep_kernel(has_vres, tm, *refs):
    if has_vres:
        (za_ref, h_ref, vfirst_ref, mu_ref, w0_ref, w2_ref, a0_ref, a2_ref, g2_ref, kk_ref, ka_ref,
         rk_ref, bd_ref, shift_ref, tri_ref, vw1_ref, vmu_ref, vw2_ref, vb_ref,
         rt_ref, at_ref, v_ref, bT_ref, kT_ref, pcT_ref, g_ref, bonus_ref, carry, carry_v) = refs
    else:
        (za_ref, mu_ref, w0_ref, w2_ref, a0_ref, a2_ref, g2_ref, kk_ref, ka_ref, rk_ref, bd_ref,
         shift_ref, tri_ref,
         rt_ref, at_ref, v_ref, bT_ref, kT_ref, pcT_ref, g_ref, bonus_ref, carry) = refs
    t = pl.program_id(1)

    @pl.when(t == 0)
    def _():
        carry[...] = jnp.zeros_like(carry)
        if has_vres:
            carry_v[...] = jnp.zeros_like(carry_v)

    row = lax.broadcasted_iota(i32, (tm, 1), 0)
    zb = za_ref[...]
    z = zb.astype(f32)
    shifted = jnp.dot(shift_ref[...], zb, preferred_element_type=f32)
    prev = jnp.where(row == 0, carry[0:1, :], shifted)
    carry[0:1, :] = z[tm - 1:tm, :]
    zs = z + (prev - z) * mu_ref[...]
    w = A_WIDTH
    r, k, v = zs[:, 0:w], zs[:, w:2 * w], zs[:, 2 * w:3 * w]
    lo = zs[:, 3 * w:3 * w + LANES]
    glo = zs[:, 3 * w + LANES:3 * w + 2 * LANES]

    if has_vres:
        hv = jnp.dot(h_ref[...], vw1_ref[...], preferred_element_type=f32)
        prev_v = jnp.where(row == 0, carry_v[0:1, :], pltpu.roll(hv, 1, axis=0))
        carry_v[0:1, :] = hv[tm - 1:tm, :]
        vlo = hv + (prev_v - hv) * vmu_ref[...]
        sg = _sigmoid(vb_ref[...] + _bdot(vlo, vw2_ref[...]))
        v = v + (vfirst_ref[...].astype(f32) - v) * sg

    wl = w0_ref[...] + _bdot(jnp.tanh(lo), w2_ref[...])
    yy = -wl
    softplus = jnp.maximum(yy, 0.0) + jnp.log(1.0 + jnp.exp(-jnp.abs(yy)))
    ld = -jnp.exp(-softplus - 0.5)
    a = _sigmoid(a0_ref[...] + _bdot(lo, a2_ref[...]))
    g = _bdot(_sigmoid(glo), g2_ref[...])
    bd = bd_ref[...]
    kk = k * kk_ref[...]
    kk = kk / jnp.maximum(jnp.sqrt(_head_sum(kk * kk, bd)), 1e-12)
    kp = k * (1.0 + (a - 1.0) * ka_ref[...])
    bonus = _head_sum(r * kp * rk_ref[...], bd) * v

    tri = tri_ref[...]
    ld_hi = ld.astype(bf16)
    res = ld - ld_hi.astype(f32)
    ld_mid = res.astype(bf16)
    ld_lo = (res - ld_mid.astype(f32)).astype(bf16)
    cum = (jnp.dot(tri, ld_hi, preferred_element_type=f32) + jnp.dot(tri, ld_mid, preferred_element_type=f32)
           + jnp.dot(tri, ld_lo, preferred_element_type=f32))
    p_inc = jnp.exp(cum)
    p_inv = jnp.exp(-cum)
    ends = [cum[(c + 1) * WKV_CHUNK - 1:(c + 1) * WKV_CHUNK, :] for c in range(tm // WKV_CHUNK)]
    cum_end = ends[0]
    for c in range(1, tm // WKV_CHUNK):
        cum_end = jnp.where(row >= c * WKV_CHUNK, ends[c], cum_end)
    cum_end = jnp.broadcast_to(cum_end, (tm, w))

    rt_ref[...] = (r * p_inc).astype(bf16)
    at_ref[...] = (-kk * jnp.exp(cum - ld)).astype(bf16)
    v_ref[...] = v.astype(bf16)
    bT_ref[...] = (kk * a * p_inv).T.astype(bf16)
    kT_ref[...] = (kp * p_inv).T.astype(bf16)
    pcT_ref[...] = jnp.exp(cum_end).T
    g_ref[...] = g.astype(bf16)
    bonus_ref[...] = bonus.astype(bf16)


def _wkv_prep(za, bsz, T, p, vres, tm=256):
    has_vres = vres is not None
    w = A_WIDTH
    tok = lambda cols: pl.BlockSpec((None, tm, cols), lambda b, t: (b, t, 0))
    par = lambda r_, c_: pl.BlockSpec((r_, c_), lambda b, t: (0, 0))
    ins = [za.reshape(bsz, T, A_COLS)]
    specs = [tok(A_COLS)]
    if has_vres:
        ins += [vres["h"].reshape(bsz, T, D_MODEL), vres["vfirst"]]
        specs += [tok(D_MODEL), tok(w)]
    ti = jnp.arange(tm)
    shift = (ti[:, None] == ti[None, :] + 1).astype(bf16)
    tri = ((ti[:, None] >= ti[None, :]) & (ti[:, None] // WKV_CHUNK == ti[None, :] // WKV_CHUNK)).astype(bf16)
    ins += [p["mu"], p["w0"], p["w2"], p["a0"], p["a2"], p["g2"], p["k_k"], p["k_a"], p["r_k"], p["bd"],
            shift, tri]
    specs += [par(1, A_COLS), par(1, w), par(LANES, w), par(1, w), par(LANES, w), par(LANES, w),
              par(1, w), par(1, w), par(1, w), par(LANES, LANES), par(tm, tm), par(tm, tm)]
    if has_vres:
        ins += [vres["w1"], vres["mu"], vres["w2"], vres["b"]]
        specs += [par(D_MODEL, LANES), par(1, LANES), par(LANES, w), par(1, w)]
    tr = pl.BlockSpec((None, w, tm), lambda b, t: (b, 0, t))
    out_shape = [jax.ShapeDtypeStruct((bsz, T, w), bf16)] * 3 + \
                [jax.ShapeDtypeStruct((bsz, w, T), bf16)] * 2 + \
                [jax.ShapeDtypeStruct((bsz, w, T), f32)] + \
                [jax.ShapeDtypeStruct((bsz, T, w), bf16)] * 2
    out_specs = [tok(w)] * 3 + [tr] * 3 + [tok(w)] * 2
    scratch = [pltpu.VMEM((8, A_COLS), f32)]
    if has_vres:
        scratch.append(pltpu.VMEM((8, LANES), f32))
    return pl.pallas_call(
        functools.partial(_wkv_prep_kernel, has_vres, tm),
        out_shape=out_shape,
        grid=(bsz, T // tm),
        in_specs=specs,
        out_specs=out_specs,
        scratch_shapes=scratch,
        compiler_params=_cparams("parallel", "arbitrary"),
        name="wkv_prep",
    )(*ins)


def _wkv_scan_kernel(rt_ref, at_ref, v_ref, bT_ref, kT_ref, pcT_ref, y_ref, st_ref):
    C = WKV_CHUNK

    @pl.when(pl.program_id(1) == 0)
    def _():
        st_ref[...] = jnp.zeros_like(st_ref)

    head0 = lax.broadcasted_iota(i32, (1, LANES), 1) < HEAD
    key0 = lax.broadcasted_iota(i32, (LANES, 1), 0) < HEAD
    same_head = head0 == key0
    ri = lax.broadcasted_iota(i32, (C, C), 0)
    ci = lax.broadcasted_iota(i32, (C, C), 1)
    strict = ri > ci
    lower2 = jnp.concatenate([ri >= ci, ri >= ci], axis=1)
    eye = (ri == ci).astype(f32)

    fdot = lambda a, b: jnp.dot(a, b, preferred_element_type=f32)
    hms = (head0, jnp.logical_not(head0))

    def pair_group(gi, carry_):
        pis = [gi * SCAN_GROUP + q for q in range(SCAN_GROUP)]
        c0s = [pl.multiple_of(pi * LANES, LANES) for pi in pis]
        ars = [jnp.concatenate([at_ref[:, pl.ds(c0, LANES)], rt_ref[:, pl.ds(c0, LANES)]], axis=0) for c0 in c0s]
        vs = [v_ref[:, pl.ds(c0, LANES)] for c0 in c0s]
        bks = [jnp.concatenate([bT_ref[pl.ds(c0, LANES), :], kT_ref[pl.ds(c0, LANES), :]], axis=1) for c0 in c0s]
        sts = [st_ref[pi] for pi in pis]
        xss = [fdot(ar, st.astype(bf16)) for ar, st in zip(ars, sts)]
        heads = [(q, hm) for q in range(SCAN_GROUP) for hm in hms]
        grams = [fdot(jnp.where(hm, ars[q], 0), bks[q]) for q, hm in heads]
        npows = [jnp.where(strict, gm[0:C, 0:C], 0.0) for gm in grams]
        a_aks = [jnp.where(strict, gm[0:C, C:2 * C], 0.0).astype(bf16) for gm in grams]
        a_rs = [jnp.where(lower2, gm[C:2 * C, :], 0.0).astype(bf16) for gm in grams]
        xins = [xss[q][0:C] + fdot(a_ak, vs[q]) for (q, _), a_ak in zip(heads, a_aks)]
        winvs = [eye + n for n in npows]
        s = 2
        while s < C:
            npows = [_bdot(n, n) for n in npows]
            winvs = [w + _bdot(w, n) for w, n in zip(winvs, npows)]
            s *= 2
        us = [_bdot(w, x) for w, x in zip(winvs, xins)]
        for q in range(SCAN_GROUP):
            uv = jnp.concatenate([jnp.where(head0, us[2 * q], us[2 * q + 1]).astype(bf16), vs[q]], axis=0)
            y_ref[:, pl.ds(c0s[q], LANES)] = xss[q][C:2 * C] + jnp.where(
                head0, fdot(a_rs[2 * q], uv), fdot(a_rs[2 * q + 1], uv))
            pc = pcT_ref[pl.ds(c0s[q], LANES), :]
            bkp = (bks[q].astype(f32) * jnp.concatenate([pc, pc], axis=1)).astype(bf16)
            st_ref[pis[q]] = sts[q] * pc + jnp.where(same_head, fdot(bkp, uv), 0.0)
        return carry_

    lax.fori_loop(0, A_WIDTH // LANES // SCAN_GROUP, pair_group, 0)


def _wkv_scan(rt, at, v, bT, kT, pcT, bsz, T):
    C = WKV_CHUNK
    w = A_WIDTH
    tok = pl.BlockSpec((None, C, w), lambda b, t: (b, t, 0))
    tr = pl.BlockSpec((None, w, C), lambda b, t: (b, 0, t))
    return pl.pallas_call(
        _wkv_scan_kernel,
        out_shape=jax.ShapeDtypeStruct((bsz, T, w), f32),
        grid=(bsz, T // C),
        in_specs=[tok, tok, tok, tr, tr, tr],
        out_specs=tok,
        scratch_shapes=[pltpu.VMEM((w // LANES, LANES, LANES), f32)],
        compiler_params=_cparams("parallel", "arbitrary"),
        name="wkv_scan",
    )(rt, at, v, bT, kT, pcT)


def _attn_kernel(q_ref, k_ref, v_ref, cos_ref, sin_ref, o_ref,
                 qs, kall, vall, ktails, vtails, m_acc, num_acc, den_acc):
    t = pl.program_id(1)
    g = pl.program_id(2)
    TT = ATT_TILE
    blk = ATT_BLOCK
    width = q_ref.shape[-1]
    lane = lax.broadcasted_iota(i32, (1, width), 1)
    first_half = (lane % HEAD) < (HEAD // 2)
    cos = cos_ref[...]
    sin = sin_ref[...]

    def rope(z):
        partner = jnp.where(first_half, pltpu.roll(z, width - HEAD // 2, axis=1),
                            pltpu.roll(z, HEAD // 2, axis=1))
        return z * cos + partner * sin

    halves = width // LANES

    def put(ref, r0, val):
        for c in range(halves):
            ref[c, r0:r0 + val.shape[0], :] = val[:, c * LANES:(c + 1) * LANES]

    def take(ref, r0, n):
        return jnp.concatenate([ref[c, r0:r0 + n, :] for c in range(halves)], axis=1)

    put(qs, 0, rope(q_ref[...].astype(f32)) * (HEAD ** -0.5))
    put(kall, TT, rope(k_ref[...].astype(f32)))
    put(vall, TT, v_ref[...].astype(f32))

    qi = lax.broadcasted_iota(i32, (blk, 2 * blk), 0)
    kj = lax.broadcasted_iota(i32, (blk, 2 * blk), 1)
    dist = qi + blk - kj
    band = (dist >= 0) & (dist <= ATT_SPAN)
    cur_keys = kj >= blk
    head_masks = [(lane // HEAD) == h for h in range(width // HEAD)]

    def group(gi, d):
        tail = blk * d
        ktail, vtail = ktails[gi], vtails[gi]

        @pl.when(t == 0)
        def _():
            put(kall, TT - tail, jnp.zeros((tail, width), f32))
            put(vall, TT - tail, jnp.zeros((tail, width), f32))

        @pl.when(t > 0)
        def _():
            put(kall, TT - tail, ktail[...])
            put(vall, TT - tail, vtail[...])

        def unit(u, carry_):
            r = u % d
            nl = u // d
            start = r + tail * nl
            rows = lambda base: pl.ds(base + start, blk, stride=d)
            ld = lambda ref, base: jnp.concatenate([ref[c, rows(base), :] for c in range(halves)], axis=1)

            def st(ref, val):
                for c in range(halves):
                    ref[c, rows(0), :] = val[:, c * LANES:(c + 1) * LANES]

            q = ld(qs, 0)
            kcat = jnp.concatenate([ld(kall, TT - tail), ld(kall, TT)], axis=0).astype(bf16)
            vcat = jnp.concatenate([ld(vall, TT - tail), ld(vall, TT)], axis=0).astype(bf16)
            has_prev = jnp.logical_or(t > 0, nl > 0)
            mask = band & (cur_keys | has_prev)
            ss = [lax.dot_general(jnp.where(hm, q, 0.0).astype(bf16), kcat, (((1,), (1,)), ((), ())),
                                  preferred_element_type=f32) for hm in head_masks]
            ss = [jnp.where(mask, s, -jnp.inf) for s in ss]
            ms = [jnp.max(s, axis=-1, keepdims=True) for s in ss]
            es = [jnp.exp(s - m) for s, m in zip(ss, ms)]
            dens = [jnp.sum(e, axis=-1, keepdims=True) for e in es]
            pvs = [jnp.dot(e.astype(bf16), vcat, preferred_element_type=f32) for e in es]
            num, m_full, den_full = pvs[0], ms[0], dens[0]
            for hm, pv, m, den in zip(head_masks[1:], pvs[1:], ms[1:], dens[1:]):
                num = jnp.where(hm, pv, num)
                m_full = jnp.where(hm, m, m_full)
                den_full = jnp.where(hm, den, den_full)
            if gi == 0:
                st(m_acc, m_full)
                st(num_acc, num)
                st(den_acc, den_full)
            else:
                m_old = ld(m_acc, 0)
                m_new = jnp.maximum(m_old, m_full)
                w_old = jnp.exp(m_old - m_new)
                w_new = jnp.exp(m_full - m_new)
                st(m_acc, m_new)
                st(num_acc, ld(num_acc, 0) * w_old + num * w_new)
                st(den_acc, ld(den_acc, 0) * w_old + den_full * w_new)
            return carry_

        lax.fori_loop(0, TT // blk, unit, 0)
        ktail[...] = take(kall, 2 * TT - tail, tail)
        vtail[...] = take(vall, 2 * TT - tail, tail)

    for gi, d in enumerate(DILATIONS):
        pl.when(g == gi)(functools.partial(group, gi, d))

    @pl.when(g == len(DILATIONS) - 1)
    def _():
        o_ref[...] = (take(num_acc, 0, TT) / take(den_acc, 0, TT)).astype(o_ref.dtype)


def _attention(zb, rope_tab, bsz, T):
    TT = ATT_TILE
    width = 4 * HEAD
    zv = zb.reshape(bsz, T, B_COLS)
    col = lambda off: pl.BlockSpec((None, TT, width), lambda b, t, g: (b, t, off + g))
    tab = lambda off: pl.BlockSpec((TT, width), lambda b, t, g: (t, off))
    tails = [pltpu.VMEM((ATT_BLOCK * d, width), f32) for d in DILATIONS]
    tiled = lambda rows: pltpu.VMEM((width // LANES, rows, LANES), f32)
    out = pl.pallas_call(
        _attn_kernel,
        out_shape=jax.ShapeDtypeStruct((bsz, T, width), bf16),
        grid=(bsz, T // TT, len(DILATIONS)),
        in_specs=[col(0), col(3), col(6), tab(0), tab(1)],
        out_specs=pl.BlockSpec((None, TT, width), lambda b, t, g: (b, t, 0)),
        scratch_shapes=[tiled(TT), tiled(2 * TT), tiled(2 * TT), tails, list(tails),
                        tiled(TT), tiled(TT), tiled(TT)],
        compiler_params=_cparams("parallel", "arbitrary", "arbitrary"),
        name="dilated_attn",
    )(zv, zv, zv, rope_tab, rope_tab)
    return out.reshape(bsz * T, width)


def _sgu_kernel(tm, zc_ref, lnw_ref, lnb_ref, ws_ref, bs_ref, o_ref):
    ch = SGU_CHUNK
    z = zc_ref[...].astype(f32)
    ge = 0.5 * z * (1.0 + lax.erf(z * (2.0 ** -0.5)))
    u = ge[:, :D_MODEL]
    v = ge[:, D_MODEL:]
    mean = jnp.mean(v, axis=-1, keepdims=True)
    var = jnp.mean(jnp.square(v - mean), axis=-1, keepdims=True)
    vn = ((v - mean) * lax.rsqrt(var + LN_EPS) * lnw_ref[...] + lnb_ref[...]).astype(bf16)
    ri = lax.broadcasted_iota(i32, (ch, ch), 0)
    ci = lax.broadcasted_iota(i32, (ch, ch), 1)
    causal = ri >= ci
    for g in range(SGU_GROUPS):
        wg = jnp.where(causal, ws_ref[g], 0.0).astype(bf16)
        bias = bs_ref[g]
        for c in range(tm // ch):
            rows = slice(c * ch, (c + 1) * ch)
            cols = slice(g * LANES, (g + 1) * LANES)
            mixed = jnp.dot(wg, vn[rows, cols], preferred_element_type=f32) + bias
            o_ref[rows, cols] = (u[rows, cols] * mixed).astype(o_ref.dtype)


def _sgu(zc, lnw, lnb, ws, bs_b, tm=512):
    n = zc.shape[0]
    return pl.pallas_call(
        functools.partial(_sgu_kernel, tm),
        out_shape=jax.ShapeDtypeStruct((n, D_MODEL), bf16),
        grid=(n // tm,),
        in_specs=[pl.BlockSpec((tm, C_COLS), lambda i: (i, 0)),
                  pl.BlockSpec((1, D_MODEL), lambda i: (0, 0)),
                  pl.BlockSpec((1, D_MODEL), lambda i: (0, 0)),
                  pl.BlockSpec((SGU_GROUPS, SGU_CHUNK, SGU_CHUNK), lambda i: (0, 0, 0)),
                  pl.BlockSpec((SGU_GROUPS, SGU_CHUNK, LANES), lambda i: (0, 0, 0))],
        out_specs=pl.BlockSpec((tm, D_MODEL), lambda i: (i, 0)),
        compiler_params=_cparams("parallel"),
        name="sgu",
    )(zc, lnw, lnb, ws, bs_b)


def _merge_kernel(x_ref, y_ref, g_ref, bonus_ref, yb_ref, yc_ref, zg_ref, lnxw_ref, lnxb_ref, gb_ref, wpa_ref, wpb_ref, wpc_ref, wout_ref,
                  nf_ref, bd_ref, rwh_ref, rwl_ref, rb_ref, x1_ref, h2_ref, route_ref):
    bd = bd_ref[...]
    y = y_ref[...]
    mean = _head_sum(y, bd) * (1.0 / HEAD)
    dy = y - mean
    var = _head_sum(dy * dy, bd) * (1.0 / HEAD)
    yn = dy * lax.rsqrt(var + GN_EPS) * lnxw_ref[...] + lnxb_ref[...] + bonus_ref[...].astype(f32)
    ya = (yn * g_ref[...].astype(f32)).astype(bf16)

    gates = _sigmoid(zg_ref[...].astype(f32) + gb_ref[...])
    d = D_MODEL
    merged = (gates[:, 0:d] * jnp.dot(ya, wpa_ref[...], preferred_element_type=f32)
              + gates[:, d:2 * d] * jnp.dot(yb_ref[...], wpb_ref[...], preferred_element_type=f32)
              + gates[:, 2 * d:3 * d] * jnp.dot(yc_ref[...], wpc_ref[...], preferred_element_type=f32))
    x1 = x_ref[...] + _bdot(merged, wout_ref[...])
    x1_ref[...] = x1
    hn = x1 * lax.rsqrt(jnp.mean(x1 * x1, axis=-1, keepdims=True) + RMS_EPS) * nf_ref[...]
    h2_ref[...] = hn

    hh = hn.astype(bf16)
    hl = (hn - hh.astype(f32)).astype(bf16)
    logits = (jnp.dot(hh, rwh_ref[...], preferred_element_type=f32)
              + jnp.dot(hl, rwh_ref[...], preferred_element_type=f32)
              + jnp.dot(hh, rwl_ref[...], preferred_element_type=f32)) + rb_ref[...]
    lane_i = lax.broadcasted_iota(i32, logits.shape, 1)
    lane = lane_i.astype(f32)
    lane_grp = ((lane_i - N_GROUPS) // EXPERTS_PER_GROUP).astype(f32)
    neg = jnp.float32(-1e30)
    big = jnp.float32(1e9)
    is_g = lane_i < N_GROUPS
    gl = jnp.where(is_g, logits, neg)
    gmax = jnp.max(gl, axis=-1, keepdims=True)
    gsel = jnp.min(jnp.where(gl == gmax, lane, big), axis=-1, keepdims=True)
    gsum = jnp.sum(jnp.where(is_g, jnp.exp(gl - gmax), 0.0), axis=-1, keepdims=True)
    pg = 1.0 / gsum
    emask = (lane_i >= N_GROUPS) & (lane_i < N_GROUPS + N_EXPERTS) & (lane_grp == gsel)
    el = jnp.where(emask, logits, neg)
    t1 = jnp.max(el, axis=-1, keepdims=True)
    i1 = jnp.min(jnp.where(el == t1, lane, big), axis=-1, keepdims=True)
    el2 = jnp.where(lane == i1, neg, el)
    t2 = jnp.max(el2, axis=-1, keepdims=True)
    i2 = jnp.min(jnp.where(el2 == t2, lane, big), axis=-1, keepdims=True)
    e21 = jnp.exp(t2 - t1)
    wa = pg / (1.0 + e21)
    wb = pg * e21 / (1.0 + e21)
    route = jnp.where(lane_i == 0, i1 - N_GROUPS,
                      jnp.where(lane_i == 1, i2 - N_GROUPS,
                                jnp.where(lane_i == 2, wa, jnp.where(lane_i == 3, wb, 0.0))))
    route_ref[...] = route


def _merge(x, y, g, bonus, yb, yc, zg, p, tm=256):
    n = x.shape[0]
    d = D_MODEL
    tok = lambda c: pl.BlockSpec((tm, c), lambda i: (i, 0))
    par = lambda r_, c_: pl.BlockSpec((r_, c_), lambda i: (0, 0))
    return pl.pallas_call(
        _merge_kernel,
        out_shape=[jax.ShapeDtypeStruct((n, d), f32), jax.ShapeDtypeStruct((n, d), f32),
                   jax.ShapeDtypeStruct((n, LANES), f32)],
        grid=(n // tm,),
        in_specs=[tok(d), tok(d), tok(d), tok(d), tok(256), tok(d), tok(G_COLS), par(1, d), par(1, d), par(1, G_COLS), par(d, d), par(256, d),
                  par(d, d), par(d, d), par(1, d), par(LANES, LANES), par(d, LANES), par(d, LANES),
                  par(1, LANES)],
        out_specs=[tok(d), tok(d), tok(LANES)],
        compiler_params=_cparams("parallel"),
        name="merge_router",
    )(x, y, g, bonus, yb, yc, zg, p["lnx_w"], p["lnx_b"], p["gate_b"], p["w_pa"], p["w_pb"],
      p["w_pc"], p["w_out"], p["norm_ffn"], p["bd"], p["rw_hi"], p["rw_lo"], p["rb"])


def _moe_kernel(blk_e_ref, nv_ref, nused_ref, tok_ref, tok_next_ref, dst_ref, h2_hbm, w1_ref, w3_ref, w2_ref,
                out_hbm, xbuf, ybuf, sem_in, sem_out):
    j = pl.program_id(0)
    rows = MOE_BLOCK
    nused = nused_ref[0]
    slot = j % 2

    def start_gather(idx_ref, buf):
        def body(i, c):
            pltpu.make_async_copy(h2_hbm.at[pl.ds(idx_ref[0, 0, i], 1), :],
                                  xbuf.at[buf, pl.ds(i, 1), :], sem_in.at[buf]).start()
            return c
        lax.fori_loop(0, rows, body, 0, unroll=8)

    def wait_scatter(count):
        bulk = pl.multiple_of((count // 8) * 8, 8)

        @pl.when(bulk > 0)
        def _():
            pltpu.make_async_copy(ybuf.at[pl.ds(0, bulk), :], out_hbm.at[pl.ds(0, bulk), :], sem_out).wait()

        def body(i, c):
            pltpu.make_async_copy(ybuf.at[pl.ds(0, 1), :], out_hbm.at[pl.ds(0, 1), :], sem_out).wait()
            return c
        lax.fori_loop(0, count - bulk, body, 0)

    @pl.when(j == 0)
    def _():
        start_gather(tok_ref, 0)

    @pl.when(j + 1 < nused)
    def _():
        start_gather(tok_next_ref, 1 - slot)

    @pl.when(j < nused)
    def _():
        pltpu.make_async_copy(h2_hbm.at[pl.ds(0, rows), :], xbuf.at[slot], sem_in.at[slot]).wait()
        x = xbuf[slot].astype(bf16)
        h1 = jnp.dot(x, w1_ref[...], preferred_element_type=f32)
        h3 = jnp.dot(x, w3_ref[...], preferred_element_type=f32)
        hid = (h1 * _sigmoid(h1) * h3).astype(bf16)
        y = jnp.dot(hid, w2_ref[...], preferred_element_type=f32)

        @pl.when(j > 0)
        def _():
            wait_scatter(nv_ref[j - 1])

        ybuf[...] = y
        nv = nv_ref[j]

        def body(i, c):
            pltpu.make_async_copy(ybuf.at[pl.ds(i, 1), :],
                                  out_hbm.at[pl.ds(dst_ref[0, 0, i], 1), :], sem_out).start()
            return c

        @pl.when(nv == rows)
        def _():
            lax.fori_loop(0, rows, body, 0, unroll=8)

        @pl.when(nv < rows)
        def _():
            lax.fori_loop(0, nv, body, 0)

        @pl.when(j == nused - 1)
        def _():
            wait_scatter(nv)


def _moe(h2, blk_e, nv, nused, tok, dst, w1, w3, w2, nblk, n_out_rows):
    d = D_MODEL
    idx_spec = lambda f: pl.BlockSpec((1, 1, MOE_BLOCK), f, memory_space=pltpu.SMEM)
    wspec = lambda r_, c_: pl.BlockSpec((None, r_, c_), lambda j, be, nv_, nu: (be[j], 0, 0))
    grid_spec = pltpu.PrefetchScalarGridSpec(
        num_scalar_prefetch=3,
        grid=(nblk,),
        in_specs=[idx_spec(lambda j, be, nv_, nu: (j, 0, 0)),
                  idx_spec(lambda j, be, nv_, nu: (jnp.minimum(j + 1, nblk - 1), 0, 0)),
                  idx_spec(lambda j, be, nv_, nu: (j, 0, 0)),
                  pl.BlockSpec(memory_space=pl.ANY),
                  wspec(d, EXPERT_FF), wspec(d, EXPERT_FF), wspec(EXPERT_FF, d)],
        out_specs=pl.BlockSpec(memory_space=pl.ANY),
        scratch_shapes=[pltpu.VMEM((2, MOE_BLOCK, d), f32), pltpu.VMEM((MOE_BLOCK, d), f32),
                        pltpu.SemaphoreType.DMA((2,)), pltpu.SemaphoreType.DMA(())],
    )
    return pl.pallas_call(
        _moe_kernel,
        out_shape=jax.ShapeDtypeStruct((n_out_rows, d), f32),
        grid_spec=grid_spec,
        compiler_params=_cparams("arbitrary"),
        name="moe_experts",
    )(blk_e, nv, nused, tok, tok, dst, h2, w1, w3, w2)


def _dispatch(route, n):
    nk = 2 * n
    flat_e = route[:, 0:2].astype(i32).reshape(-1)
    order = jnp.argsort(flat_e, stable=True).astype(i32)
    counts = jnp.sum(flat_e[:, None] == jnp.arange(N_EXPERTS, dtype=i32)[None, :], axis=0, dtype=i32)
    padded = (counts + MOE_BLOCK - 1) // MOE_BLOCK * MOE_BLOCK
    pend = jnp.cumsum(padded)
    pstart = pend - padded
    cstart = jnp.cumsum(counts) - counts
    nblk = nk // MOE_BLOCK + N_EXPERTS
    blk_start = jnp.arange(nblk, dtype=i32) * MOE_BLOCK
    blk_e = jnp.minimum(jnp.sum(pend[None, :] <= blk_start[:, None], axis=1, dtype=i32), N_EXPERTS - 1)
    off = blk_start - pstart[blk_e]
    nv = jnp.clip(counts[blk_e] - off, 0, MOE_BLOCK).astype(i32)
    within = jnp.arange(MOE_BLOCK, dtype=i32)[None, :]
    src = jnp.clip((cstart[blk_e] + off)[:, None] + within, 0, nk - 1)
    assign = order[src]
    valid = within < nv[:, None]
    tok = jnp.where(valid, assign // 2, 0)
    dst = jnp.where(valid, (assign % 2) * n + assign // 2, 0)
    nused = (pend[-1] // MOE_BLOCK).astype(i32).reshape(1)
    return (blk_e, nv, nused, tok.reshape(nblk, 1, MOE_BLOCK), dst.reshape(nblk, 1, MOE_BLOCK), nblk)


def _combine_kernel(emit_x, x1_ref, ya_ref, yb_ref, route_ref, g_ref, *outs):
    route = route_ref[...]
    x2 = x1_ref[...] + route[:, 2:3] * ya_ref[...] + route[:, 3:4] * yb_ref[...]
    hn = x2 * lax.rsqrt(jnp.mean(x2 * x2, axis=-1, keepdims=True) + RMS_EPS) * g_ref[...]
    if emit_x:
        outs[0][...] = x2
        outs[1][...] = hn.astype(outs[1].dtype)
    else:
        outs[0][...] = hn.astype(outs[0].dtype)


def _combine(x1, out2, route, g, emit_x, tm=512):
    n, d = x1.shape
    nt = n // tm
    tok = lambda c: pl.BlockSpec((tm, c), lambda i: (i, 0))
    if emit_x:
        out_shape = [jax.ShapeDtypeStruct((n, d), f32), jax.ShapeDtypeStruct((n, d), bf16)]
        out_specs = [tok(d), tok(d)]
    else:
        out_shape = [jax.ShapeDtypeStruct((n, d), f32)]
        out_specs = [tok(d)]
    return pl.pallas_call(
        functools.partial(_combine_kernel, emit_x),
        out_shape=out_shape,
        grid=(nt,),
        in_specs=[tok(d), tok(d), pl.BlockSpec((tm, d), lambda i: (i + nt, 0)), tok(LANES),
                  pl.BlockSpec((1, d), lambda i: (0, 0))],
        out_specs=out_specs,
        compiler_params=_cparams("parallel"),
        name="moe_combine",
    )(x1, out2, out2, route, g.reshape(1, d))


def _rope_table(T):
    pos = jnp.arange(T, dtype=f32)
    inv_freq = ROPE_THETA ** (-jnp.arange(0, HEAD, 2, dtype=f32) / HEAD)
    ang = pos[:, None] * inv_freq[None, :]
    cos, sin = jnp.cos(ang), jnp.sin(ang)
    cos_h = jnp.concatenate([cos, cos], axis=-1)
    sin_h = jnp.concatenate([-sin, sin], axis=-1)
    return jnp.concatenate([jnp.tile(cos_h, (1, 4)), jnp.tile(sin_h, (1, 4))], axis=-1)


def _pad_rows(w, rows, at=0):
    out = jnp.zeros((rows, w.shape[1]), w.dtype)
    return out.at[at:at + w.shape[0]].set(w)


def kernel(x, norm_mix, w_in, mu_a, w_decay0, w_decay2, a0, a2, g2, k_k, k_a, r_k, lnx_w, lnx_b, vres_mu, vres_w1, vres_w2, vres_b, sgu_ln_w, sgu_ln_b, sgu_ws, sgu_bs, gate_b, w_pa, w_pb, w_pc, w_out, norm_ffn, router_grp, router_grp_b, router_exp, router_exp_b, moe_w1, moe_w3, moe_w2, norm_final):
    bsz, T, d = x.shape
    n = bsz * T
    depth = w_in.shape[0]
    rope_tab = _rope_table(T)
    head_id = jnp.arange(LANES) // HEAD
    bd = (head_id[:, None] == head_id[None, :]).astype(bf16)
    row = lambda v: v.reshape(1, -1).astype(f32)

    xf = x.reshape(n, d)
    h = _rmsnorm(xf, norm_mix[0], bf16)
    v_first = None
    for l in range(depth):
        wl = w_in[l].astype(bf16)
        c0, c1, c2 = A_COLS, A_COLS + B_COLS, A_COLS + B_COLS + C_COLS
        za = _proj(h, wl[:, :c0], A_COLS // 2, "proj_a")
        zb = _proj(h, wl[:, c0:c1], B_COLS // 2, "proj_b")
        zc = _proj(h, wl[:, c1:c2], C_COLS // 2, "proj_c")
        zg = _proj(h, wl[:, c2:], G_COLS // 2, "proj_g")

        pa = dict(mu=row(mu_a[l]), w0=row(w_decay0[l]), w2=_pad_rows(w_decay2[l], LANES, 0).astype(bf16),
                  a0=row(a0[l]), a2=_pad_rows(a2[l], LANES, 64).astype(bf16), g2=g2[l].astype(bf16),
                  k_k=row(k_k[l]), k_a=row(k_a[l]), r_k=row(r_k[l]), bd=bd)
        vres = None
        if l > 0:
            w1p = jnp.zeros((d, LANES), f32).at[:, :vres_w1.shape[2]].set(vres_w1[l - 1]).astype(bf16)
            mup = jnp.zeros((1, LANES), f32).at[0, :vres_mu.shape[1]].set(vres_mu[l - 1])
            vres = dict(h=h, vfirst=v_first, w1=w1p, mu=mup,
                        w2=_pad_rows(vres_w2[l - 1], LANES, 0).astype(bf16), b=row(vres_b[l - 1]))
        rt, at, v, bT, kT, pcT, g, bonus = _wkv_prep(za, bsz, T, pa, vres)
        if l == 0:
            v_first = v
        y = _wkv_scan(rt, at, v, bT, kT, pcT, bsz, T).reshape(n, d)

        yb = _attention(zb, rope_tab, bsz, T)

        bs_b = jnp.broadcast_to(sgu_bs[l][:, :, None], (SGU_GROUPS, SGU_CHUNK, LANES)).astype(f32)
        yc = _sgu(zc, row(sgu_ln_w[l]), row(sgu_ln_b[l]), sgu_ws[l], bs_b)

        rw = jnp.zeros((d, LANES), f32).at[:, :N_GROUPS].set(router_grp[l]) \
            .at[:, N_GROUPS:N_GROUPS + N_EXPERTS].set(router_exp[l])
        rw_hi = rw.astype(bf16)
        rw_lo = (rw - rw_hi.astype(f32)).astype(bf16)
        rb = jnp.zeros((1, LANES), f32).at[0, :N_GROUPS].set(router_grp_b[l]) \
            .at[0, N_GROUPS:N_GROUPS + N_EXPERTS].set(router_exp_b[l])
        pm = dict(lnx_w=row(lnx_w[l]), lnx_b=row(lnx_b[l]), gate_b=row(gate_b[l]), w_pa=w_pa[l].astype(bf16),
                  w_pb=w_pb[l].astype(bf16), w_pc=w_pc[l].astype(bf16), w_out=w_out[l].astype(bf16),
                  norm_ffn=row(norm_ffn[l]), bd=bd, rw_hi=rw_hi, rw_lo=rw_lo, rb=rb)
        x1, h2, route = _merge(xf, y, g.reshape(n, d), bonus.reshape(n, d), yb, yc, zg, pm)

        blk_e, nv, nused, tok, dst, nblk = _dispatch(route, n)
        out2 = _moe(h2, blk_e, nv, nused, tok, dst, moe_w1[l].astype(bf16), moe_w3[l].astype(bf16),
                    moe_w2[l].astype(bf16), nblk, 2 * n)
        if l + 1 < depth:
            xf, h = _combine(x1, out2, route, norm_mix[l + 1], True)
        else:
            (out,) = _combine(x1, out2, route, norm_final, False)
    return out.reshape(bsz, T, d)
```

```python
import functools
import math

import jax
import jax.numpy as jnp
from jax import lax
from jax.experimental import pallas as pl
from jax.experimental.pallas import tpu as pltpu

f32 = jnp.float32
bf16 = jnp.bfloat16
i32 = jnp.int32

D_MODEL = 1024
HEAD = 64
RMS_EPS = 1e-6
GN_EPS = 64e-5
LN_EPS = 1e-5
ROPE_THETA = 10000.0
A_WIDTH = D_MODEL
A_COLS = 3 * A_WIDTH + 64 + 64 + 128
B_COLS = 3 * 768
C_COLS = 2 * D_MODEL
G_COLS = 3 * D_MODEL
DILATIONS = (1, 4, 16)
ATT_BLOCK = 128
ATT_SPAN = 128
ATT_TILE = ATT_BLOCK * max(DILATIONS)
ATT_UNROLL = 2
SGU_CHUNK = 128
SGU_GROUPS = 8
N_GROUPS = 4
EXPERTS_PER_GROUP = 8
N_EXPERTS = N_GROUPS * EXPERTS_PER_GROUP
EXPERT_FF = D_MODEL // 2
MOE_BLOCK = 256
MOE_PIECE = 256
WKV_CHUNK = 128
SCAN_GROUP = 8
LANES = 128
VMEM_LIMIT = 48 * 1024 * 1024


def _cparams(*sem):
    return pltpu.CompilerParams(dimension_semantics=sem, vmem_limit_bytes=VMEM_LIMIT)


def _bdot(a, b):
    return jnp.dot(a.astype(bf16), b.astype(bf16), preferred_element_type=f32)


def _dot_hilo(a, b):
    hi = a.astype(bf16)
    lo = (a - hi.astype(f32)).astype(bf16)
    return (jnp.dot(hi, b, preferred_element_type=f32)
            + jnp.dot(lo, b, preferred_element_type=f32))


def _head_sum(x, bd):
    parts = [_dot_hilo(x[:, p * LANES:(p + 1) * LANES], bd) for p in range(x.shape[1] // LANES)]
    return jnp.concatenate(parts, axis=1)


def _sigmoid(x):
    return 1.0 / (1.0 + jnp.exp(-x))


def _rmsnorm_kernel(x_ref, g_ref, o_ref):
    x = x_ref[...]
    y = x * lax.rsqrt(jnp.mean(x * x, axis=-1, keepdims=True) + RMS_EPS)
    o_ref[...] = (y * g_ref[...]).astype(o_ref.dtype)


def _rmsnorm(x, g, out_dtype, tm=1024):
    n, d = x.shape
    return pl.pallas_call(
        _rmsnorm_kernel,
        out_shape=jax.ShapeDtypeStruct((n, d), out_dtype),
        grid=(n // tm,),
        in_specs=[pl.BlockSpec((tm, d), lambda i: (i, 0)), pl.BlockSpec((1, d), lambda i: (0, 0))],
        out_specs=pl.BlockSpec((tm, d), lambda i: (i, 0)),
        compiler_params=_cparams("parallel"),
        name="rmsnorm",
    )(x, g.reshape(1, d))


def _matmul_kernel(h_ref, w_ref, o_ref):
    o_ref[...] = jnp.dot(h_ref[...], w_ref[...], preferred_element_type=f32).astype(o_ref.dtype)


def _proj(h, w, tn, name, tm=1024):
    n, k = h.shape
    m = w.shape[1]
    return pl.pallas_call(
        _matmul_kernel,
        out_shape=jax.ShapeDtypeStruct((n, m), bf16),
        grid=(m // tn, n // tm),
        in_specs=[pl.BlockSpec((tm, k), lambda j, i: (i, 0)), pl.BlockSpec((k, tn), lambda j, i: (0, j))],
        out_specs=pl.BlockSpec((tm, tn), lambda j, i: (i, j)),
        compiler_params=_cparams("parallel", "parallel"),
        name=name,
    )(h, w)


def _wkv_prep_kernel(has_vres, tm, *refs):
    if has_vres:
        (za_ref, h_ref, vfirst_ref, mu_ref, w0_ref, w2_ref, a0_ref, a2_ref, g2_ref, kk_ref, ka_ref,
         rk_ref, bd_ref, shift_ref, tri_ref, vw1_ref, vmu_ref, vw2_ref, vb_ref,
         rt_ref, at_ref, v_ref, bT_ref, kT_ref, pcT_ref, g_ref, bonus_ref, carry, carry_v) = refs
    else:
        (za_ref, mu_ref, w0_ref, w2_ref, a0_ref, a2_ref, g2_ref, kk_ref, ka_ref, rk_ref, bd_ref,
         shift_ref, tri_ref,
         rt_ref, at_ref, v_ref, bT_ref, kT_ref, pcT_ref, g_ref, bonus_ref, carry) = refs
    t = pl.program_id(1)

    @pl.when(t == 0)
    def _():
        carry[...] = jnp.zeros_like(carry)
        if has_vres:
            carry_v[...] = jnp.zeros_like(carry_v)

    row = lax.broadcasted_iota(i32, (tm, 1), 0)
    zb = za_ref[...]
    z = zb.astype(f32)
    shifted = jnp.dot(shift_ref[...], zb, preferred_element_type=f32)
    prev = jnp.where(row == 0, carry[0:1, :], shifted)
    carry[0:1, :] = z[tm - 1:tm, :]
    zs = z + (prev - z) * mu_ref[...]
    w = A_WIDTH
    r, k, v = zs[:, 0:w], zs[:, w:2 * w], zs[:, 2 * w:3 * w]
    lo = zs[:, 3 * w:3 * w + LANES]
    glo = zs[:, 3 * w + LANES:3 * w + 2 * LANES]

    if has_vres:
        hv = jnp.dot(h_ref[...], vw1_ref[...], preferred_element_type=f32)
        prev_v = jnp.where(row == 0, carry_v[0:1, :], pltpu.roll(hv, 1, axis=0))
        carry_v[0:1, :] = hv[tm - 1:tm, :]
        vlo = hv + (prev_v - hv) * vmu_ref[...]
        sg = _sigmoid(vb_ref[...] + _bdot(vlo, vw2_ref[...]))
        v = v + (vfirst_ref[...].astype(f32) - v) * sg

    wl = w0_ref[...] + _bdot(jnp.tanh(lo), w2_ref[...])
    yy = -wl
    softplus = jnp.maximum(yy, 0.0) + jnp.log(1.0 + jnp.exp(-jnp.abs(yy)))
    ld = -jnp.exp(-softplus - 0.5)
    a = _sigmoid(a0_ref[...] + _bdot(lo, a2_ref[...]))
    g = _bdot(_sigmoid(glo), g2_ref[...])
    bd = bd_ref[...]
    kk = k * kk_ref[...]
    kk = kk / jnp.maximum(jnp.sqrt(_head_sum(kk * kk, bd)), 1e-12)
    kp = k * (1.0 + (a - 1.0) * ka_ref[...])
    bonus = _head_sum(r * kp * rk_ref[...], bd) * v

    tri = tri_ref[...]
    ld_hi = ld.astype(bf16)
    res = ld - ld_hi.astype(f32)
    ld_mid = res.astype(bf16)
    ld_lo = (res - ld_mid.astype(f32)).astype(bf16)
    cum = (jnp.dot(tri, ld_hi, preferred_element_type=f32) + jnp.dot(tri, ld_mid, preferred_element_type=f32)
           + jnp.dot(tri, ld_lo, preferred_element_type=f32))
    p_inc = jnp.exp(cum)
    p_inv = jnp.exp(-cum)
    ends = [cum[(c + 1) * WKV_CHUNK - 1:(c + 1) * WKV_CHUNK, :] for c in range(tm // WKV_CHUNK)]
    cum_end = ends[0]
    for c in range(1, tm // WKV_CHUNK):
        cum_end = jnp.where(row >= c * WKV_CHUNK, ends[c], cum_end)
    cum_end = jnp.broadcast_to(cum_end, (tm, w))

    rt_ref[...] = (r * p_inc).astype(bf16)
    at_ref[...] = (-kk * jnp.exp(cum - ld)).astype(bf16)
    v_ref[...] = v.astype(bf16)
    bT_ref[...] = (kk * a * p_inv).T.astype(bf16)
    kT_ref[...] = (kp * p_inv).T.astype(bf16)
    pcT_ref[...] = jnp.exp(cum_end).T
    g_ref[...] = g.astype(bf16)
    bonus_ref[...] = bonus.astype(bf16)


def _wkv_prep(za, bsz, T, p, vres, tm=256):
    has_vres = vres is not None
    w = A_WIDTH
    tok = lambda cols: pl.BlockSpec((None, tm, cols), lambda b, t: (b, t, 0))
    par = lambda r_, c_: pl.BlockSpec((r_, c_), lambda b, t: (0, 0))
    ins = [za.reshape(bsz, T, A_COLS)]
    specs = [tok(A_COLS)]
    if has_vres:
        ins += [vres["h"].reshape(bsz, T, D_MODEL), vres["vfirst"]]
        specs += [tok(D_MODEL), tok(w)]
    ti = jnp.arange(tm)
    shift = (ti[:, None] == ti[None, :] + 1).astype(bf16)
    tri = ((ti[:, None] >= ti[None, :]) & (ti[:, None] // WKV_CHUNK == ti[None, :] // WKV_CHUNK)).astype(bf16)
    ins += [p["mu"], p["w0"], p["w2"], p["a0"], p["a2"], p["g2"], p["k_k"], p["k_a"], p["r_k"], p["bd"],
            shift, tri]
    specs += [par(1, A_COLS), par(1, w), par(LANES, w), par(1, w), par(LANES, w), par(LANES, w),
              par(1, w), par(1, w), par(1, w), par(LANES, LANES), par(tm, tm), par(tm, tm)]
    if has_vres:
        ins += [vres["w1"], vres["mu"], vres["w2"], vres["b"]]
        specs += [par(D_MODEL, LANES), par(1, LANES), par(LANES, w), par(1, w)]
    tr = pl.BlockSpec((None, w, tm), lambda b, t: (b, 0, t))
    out_shape = [jax.ShapeDtypeStruct((bsz, T, w), bf16)] * 3 + \
                [jax.ShapeDtypeStruct((bsz, w, T), bf16)] * 2 + \
                [jax.ShapeDtypeStruct((bsz, w, T), f32)] + \
                [jax.ShapeDtypeStruct((bsz, T, w), bf16)] * 2
    out_specs = [tok(w)] * 3 + [tr] * 3 + [tok(w)] * 2
    scratch = [pltpu.VMEM((8, A_COLS), f32)]
    if has_vres:
        scratch.append(pltpu.VMEM((8, LANES), f32))
    return pl.pallas_call(
        functools.partial(_wkv_prep_kernel, has_vres, tm),
        out_shape=out_shape,
        grid=(bsz, T // tm),
        in_specs=specs,
        out_specs=out_specs,
        scratch_shapes=scratch,
        compiler_params=_cparams("parallel", "arbitrary"),
        name="wkv_prep",
    )(*ins)


def _wkv_scan_kernel(rt_ref, at_ref, v_ref, bT_ref, kT_ref, pcT_ref, y_ref, st_ref):
    C = WKV_CHUNK

    @pl.when(pl.program_id(1) == 0)
    def _():
        st_ref[...] = jnp.zeros_like(st_ref)

    head0 = lax.broadcasted_iota(i32, (1, LANES), 1) < HEAD
    key0 = lax.broadcasted_iota(i32, (LANES, 1), 0) < HEAD
    same_head = head0 == key0
    ri = lax.broadcasted_iota(i32, (C, C), 0)
    ci = lax.broadcasted_iota(i32, (C, C), 1)
    strict = ri > ci
    lower2 = jnp.concatenate([ri >= ci, ri >= ci], axis=1)
    eye = (ri == ci).astype(f32)

    fdot = lambda a, b: jnp.dot(a, b, preferred_element_type=f32)
    hms = (head0, jnp.logical_not(head0))

    def pair_group(gi, carry_):
        pis = [gi * SCAN_GROUP + q for q in range(SCAN_GROUP)]
        c0s = [pl.multiple_of(pi * LANES, LANES) for pi in pis]
        ars = [jnp.concatenate([at_ref[:, pl.ds(c0, LANES)], rt_ref[:, pl.ds(c0, LANES)]], axis=0) for c0 in c0s]
        vs = [v_ref[:, pl.ds(c0, LANES)] for c0 in c0s]
        bks = [jnp.concatenate([bT_ref[pl.ds(c0, LANES), :], kT_ref[pl.ds(c0, LANES), :]], axis=1) for c0 in c0s]
        sts = [st_ref[pi] for pi in pis]
        xss = [fdot(ar, st.astype(bf16)) for ar, st in zip(ars, sts)]
        heads = [(q, hm) for q in range(SCAN_GROUP) for hm in hms]
        grams = [fdot(jnp.where(hm, ars[q], 0), bks[q]) for q, hm in heads]
        npows = [jnp.where(strict, gm[0:C, 0:C], 0.0) for gm in grams]
        a_aks = [jnp.where(strict, gm[0:C, C:2 * C], 0.0).astype(bf16) for gm in grams]
        a_rs = [jnp.where(lower2, gm[C:2 * C, :], 0.0).astype(bf16) for gm in grams]
        xins = [xss[q][0:C] + fdot(a_ak, vs[q]) for (q, _), a_ak in zip(heads, a_aks)]
        winvs = [eye + n for n in npows]
        s = 2
        while s < C:
            npows = [_bdot(n, n) for n in npows]
            winvs = [w + _bdot(w, n) for w, n in zip(winvs, npows)]
            s *= 2
        us = [_bdot(w, x) for w, x in zip(winvs, xins)]
        for q in range(SCAN_GROUP):
            uv = jnp.concatenate([jnp.where(head0, us[2 * q], us[2 * q + 1]).astype(bf16), vs[q]], axis=0)
            y_ref[:, pl.ds(c0s[q], LANES)] = xss[q][C:2 * C] + jnp.where(
                head0, fdot(a_rs[2 * q], uv), fdot(a_rs[2 * q + 1], uv))
            pc = pcT_ref[pl.ds(c0s[q], LANES), :]
            bkp = (bks[q].astype(f32) * jnp.concatenate([pc, pc], axis=1)).astype(bf16)
            st_ref[pis[q]] = sts[q] * pc + jnp.where(same_head, fdot(bkp, uv), 0.0)
        return carry_

    lax.fori_loop(0, A_WIDTH // LANES // SCAN_GROUP, pair_group, 0)


def _wkv_scan(rt, at, v, bT, kT, pcT, bsz, T):
    C = WKV_CHUNK
    w = A_WIDTH
    tok = pl.BlockSpec((None, C, w), lambda b, t: (b, t, 0))
    tr = pl.BlockSpec((None, w, C), lambda b, t: (b, 0, t))
    return pl.pallas_call(
        _wkv_scan_kernel,
        out_shape=jax.ShapeDtypeStruct((bsz, T, w), f32),
        grid=(bsz, T // C),
        in_specs=[tok, tok, tok, tr, tr, tr],
        out_specs=tok,
        scratch_shapes=[pltpu.VMEM((w // LANES, LANES, LANES), f32)],
        compiler_params=_cparams("parallel", "arbitrary"),
        name="wkv_scan",
    )(rt, at, v, bT, kT, pcT)


def _attn_kernel(q_ref, k_ref, v_ref, cos_ref, sin_ref, o_ref,
                 qs, kall, vall, ktails, vtails, m_acc, num_acc, den_acc):
    t = pl.program_id(1)
    g = pl.program_id(2)
    TT = ATT_TILE
    blk = ATT_BLOCK
    width = q_ref.shape[-1]
    lane = lax.broadcasted_iota(i32, (1, width), 1)
    first_half = (lane % HEAD) < (HEAD // 2)
    cos = cos_ref[...]
    sin = sin_ref[...]

    def rope(z):
        partner = jnp.where(first_half, pltpu.roll(z, width - HEAD // 2, axis=1),
                            pltpu.roll(z, HEAD // 2, axis=1))
        return z * cos + partner * sin

    halves = width // LANES

    def put(ref, r0, val):
        for c in range(halves):
            ref[c, r0:r0 + val.shape[0], :] = val[:, c * LANES:(c + 1) * LANES]

    def take(ref, r0, n):
        return jnp.concatenate([ref[c, r0:r0 + n, :] for c in range(halves)], axis=1)

    put(qs, 0, rope(q_ref[...].astype(f32)) * (HEAD ** -0.5))
    put(kall, TT, rope(k_ref[...].astype(f32)))
    put(vall, TT, v_ref[...].astype(f32))

    qi = lax.broadcasted_iota(i32, (blk, 2 * blk), 0)
    kj = lax.broadcasted_iota(i32, (blk, 2 * blk), 1)
    dist = qi + blk - kj
    band = (dist >= 0) & (dist <= ATT_SPAN)
    cur_keys = kj >= blk
    head_masks = [(lane // HEAD) == h for h in range(width // HEAD)]

    def group(gi, d):
        tail = blk * d
        ktail, vtail = ktails[gi], vtails[gi]

        @pl.when(t == 0)
        def _():
            put(kall, TT - tail, jnp.zeros((tail, width), f32))
            put(vall, TT - tail, jnp.zeros((tail, width), f32))

        @pl.when(t > 0)
        def _():
            put(kall, TT - tail, ktail[...])
            put(vall, TT - tail, vtail[...])

        def unit_group(ug, carry_):
            units = []
            for j in range(ATT_UNROLL):
                u = ug * ATT_UNROLL + j
                nl = u // d
                start = u % d + tail * nl
                rows = functools.partial(lambda s0, base: pl.ds(base + s0, blk, stride=d), start)
                ld = functools.partial(
                    lambda rw, ref, base: jnp.concatenate([ref[c, rw(base), :] for c in range(halves)], axis=1), rows)
                q = ld(qs, 0)
                kcat = jnp.concatenate([ld(kall, TT - tail), ld(kall, TT)], axis=0).astype(bf16)
                vcat = jnp.concatenate([ld(vall, TT - tail), ld(vall, TT)], axis=0).astype(bf16)
                mask = band & (cur_keys | jnp.logical_or(t > 0, nl > 0))
                units.append((rows, ld, q, kcat, vcat, mask))
            chains = [(un, hm) for un in units for hm in head_masks]
            ss = [lax.dot_general(jnp.where(hm, un[2], 0.0).astype(bf16), un[3], (((1,), (1,)), ((), ())),
                                  preferred_element_type=f32) for un, hm in chains]
            ss = [jnp.where(un[5], s, -jnp.inf) for (un, _), s in zip(chains, ss)]
            ms = [jnp.max(s, axis=-1, keepdims=True) for s in ss]
            es = [jnp.exp(s - m) for s, m in zip(ss, ms)]
            dens = [jnp.sum(e, axis=-1, keepdims=True) for e in es]
            pvs = [jnp.dot(e.astype(bf16), un[4], preferred_element_type=f32) for (un, _), e in zip(chains, es)]
            nh = len(head_masks)
            for j, (rows, ld, _, _, _, _) in enumerate(units):
                num, m_full, den_full = pvs[j * nh], ms[j * nh], dens[j * nh]
                for h in range(1, nh):
                    hm = head_masks[h]
                    num = jnp.where(hm, pvs[j * nh + h], num)
                    m_full = jnp.where(hm, ms[j * nh + h], m_full)
                    den_full = jnp.where(hm, dens[j * nh + h], den_full)

                def st(ref, val, rows=rows):
                    for c in range(halves):
                        ref[c, rows(0), :] = val[:, c * LANES:(c + 1) * LANES]

                if gi == 0:
                    st(m_acc, m_full)
                    st(num_acc, num)
                    st(den_acc, den_full)
                else:
                    m_old = ld(m_acc, 0)
                    m_new = jnp.maximum(m_old, m_full)
                    w_old = jnp.exp(m_old - m_new)
                    w_new = jnp.exp(m_full - m_new)
                    st(m_acc, m_new)
                    st(num_acc, ld(num_acc, 0) * w_old + num * w_new)
                    st(den_acc, ld(den_acc, 0) * w_old + den_full * w_new)
            return carry_

        lax.fori_loop(0, TT // blk // ATT_UNROLL, unit_group, 0)
        ktail[...] = take(kall, 2 * TT - tail, tail)
        vtail[...] = take(vall, 2 * TT - tail, tail)

    for gi, d in enumerate(DILATIONS):
        pl.when(g == gi)(functools.partial(group, gi, d))

    @pl.when(g == len(DILATIONS) - 1)
    def _():
        o_ref[...] = (take(num_acc, 0, TT) / take(den_acc, 0, TT)).astype(o_ref.dtype)


def _attention(zb, rope_tab, bsz, T):
    TT = ATT_TILE
    width = 4 * HEAD
    zv = zb.reshape(bsz, T, B_COLS)
    col = lambda off: pl.BlockSpec((None, TT, width), lambda b, t, g: (b, t, off + g))
    tab = lambda off: pl.BlockSpec((TT, width), lambda b, t, g: (t, off))
    tails = [pltpu.VMEM((ATT_BLOCK * d, width), f32) for d in DILATIONS]
    tiled = lambda rows: pltpu.VMEM((width // LANES, rows, LANES), f32)
    out = pl.pallas_call(
        _attn_kernel,
        out_shape=jax.ShapeDtypeStruct((bsz, T, width), bf16),
        grid=(bsz, T // TT, len(DILATIONS)),
        in_specs=[col(0), col(3), col(6), tab(0), tab(1)],
        out_specs=pl.BlockSpec((None, TT, width), lambda b, t, g: (b, t, 0)),
        scratch_shapes=[tiled(TT), tiled(2 * TT), tiled(2 * TT), tails, list(tails),
                        tiled(TT), tiled(TT), tiled(TT)],
        compiler_params=_cparams("parallel", "arbitrary", "arbitrary"),
        name="dilated_attn",
    )(zv, zv, zv, rope_tab, rope_tab)
    return out.reshape(bsz * T, width)


def _sgu_kernel(tm, zc_ref, lnw_ref, lnb_ref, ws_ref, bs_ref, o_ref):
    ch = SGU_CHUNK
    z = zc_ref[...].astype(f32)
    ge = 0.5 * z * (1.0 + lax.erf(z * (2.0 ** -0.5)))
    u = ge[:, :D_MODEL]
    v = ge[:, D_MODEL:]
    mean = jnp.mean(v, axis=-1, keepdims=True)
    var = jnp.mean(jnp.square(v - mean), axis=-1, keepdims=True)
    vn = ((v - mean) * lax.rsqrt(var + LN_EPS) * lnw_ref[...] + lnb_ref[...]).astype(bf16)
    ri = lax.broadcasted_iota(i32, (ch, ch), 0)
    ci = lax.broadcasted_iota(i32, (ch, ch), 1)
    causal = ri >= ci
    for g in range(SGU_GROUPS):
        wg = jnp.where(causal, ws_ref[g], 0.0).astype(bf16)
        bias = bs_ref[g]
        for c in range(tm // ch):
            rows = slice(c * ch, (c + 1) * ch)
            cols = slice(g * LANES, (g + 1) * LANES)
            mixed = jnp.dot(wg, vn[rows, cols], preferred_element_type=f32) + bias
            o_ref[rows, cols] = (u[rows, cols] * mixed).astype(o_ref.dtype)


def _sgu(zc, lnw, lnb, ws, bs_b, tm=512):
    n = zc.shape[0]
    return pl.pallas_call(
        functools.partial(_sgu_kernel, tm),
        out_shape=jax.ShapeDtypeStruct((n, D_MODEL), bf16),
        grid=(n // tm,),
        in_specs=[pl.BlockSpec((tm, C_COLS), lambda i: (i, 0)),
                  pl.BlockSpec((1, D_MODEL), lambda i: (0, 0)),
                  pl.BlockSpec((1, D_MODEL), lambda i: (0, 0)),
                  pl.BlockSpec((SGU_GROUPS, SGU_CHUNK, SGU_CHUNK), lambda i: (0, 0, 0)),
                  pl.BlockSpec((SGU_GROUPS, SGU_CHUNK, LANES), lambda i: (0, 0, 0))],
        out_specs=pl.BlockSpec((tm, D_MODEL), lambda i: (i, 0)),
        compiler_params=_cparams("parallel"),
        name="sgu",
    )(zc, lnw, lnb, ws, bs_b)


def _merge_kernel(x_ref, y_ref, g_ref, bonus_ref, yb_ref, yc_ref, zg_ref, lnxw_ref, lnxb_ref, gb_ref, wpa_ref, wpb_ref, wpc_ref, wout_ref,
                  nf_ref, bd_ref, rwh_ref, rwl_ref, rb_ref, x1_ref, h2_ref, route_ref):
    bd = bd_ref[...]
    y = y_ref[...]
    mean = _head_sum(y, bd) * (1.0 / HEAD)
    dy = y - mean
    var = _head_sum(dy * dy, bd) * (1.0 / HEAD)
    yn = dy * lax.rsqrt(var + GN_EPS) * lnxw_ref[...] + lnxb_ref[...] + bonus_ref[...].astype(f32)
    ya = (yn * g_ref[...].astype(f32)).astype(bf16)

    gates = _sigmoid(zg_ref[...].astype(f32) + gb_ref[...])
    d = D_MODEL
    merged = (gates[:, 0:d] * jnp.dot(ya, wpa_ref[...], preferred_element_type=f32)
              + gates[:, d:2 * d] * jnp.dot(yb_ref[...], wpb_ref[...], preferred_element_type=f32)
              + gates[:, 2 * d:3 * d] * jnp.dot(yc_ref[...], wpc_ref[...], preferred_element_type=f32))
    x1 = x_ref[...] + _bdot(merged, wout_ref[...])
    x1_ref[...] = x1
    hn = x1 * lax.rsqrt(jnp.mean(x1 * x1, axis=-1, keepdims=True) + RMS_EPS) * nf_ref[...]
    h2_ref[...] = hn

    hh = hn.astype(bf16)
    hl = (hn - hh.astype(f32)).astype(bf16)
    logits = (jnp.dot(hh, rwh_ref[...], preferred_element_type=f32)
              + jnp.dot(hl, rwh_ref[...], preferred_element_type=f32)
              + jnp.dot(hh, rwl_ref[...], preferred_element_type=f32)) + rb_ref[...]
    lane_i = lax.broadcasted_iota(i32, logits.shape, 1)
    lane = lane_i.astype(f32)
    lane_grp = ((lane_i - N_GROUPS) // EXPERTS_PER_GROUP).astype(f32)
    neg = jnp.float32(-1e30)
    big = jnp.float32(1e9)
    is_g = lane_i < N_GROUPS
    gl = jnp.where(is_g, logits, neg)
    gmax = jnp.max(gl, axis=-1, keepdims=True)
    gsel = jnp.min(jnp.where(gl == gmax, lane, big), axis=-1, keepdims=True)
    gsum = jnp.sum(jnp.where(is_g, jnp.exp(gl - gmax), 0.0), axis=-1, keepdims=True)
    pg = 1.0 / gsum
    emask = (lane_i >= N_GROUPS) & (lane_i < N_GROUPS + N_EXPERTS) & (lane_grp == gsel)
    el = jnp.where(emask, logits, neg)
    t1 = jnp.max(el, axis=-1, keepdims=True)
    i1 = jnp.min(jnp.where(el == t1, lane, big), axis=-1, keepdims=True)
    el2 = jnp.where(lane == i1, neg, el)
    t2 = jnp.max(el2, axis=-1, keepdims=True)
    i2 = jnp.min(jnp.where(el2 == t2, lane, big), axis=-1, keepdims=True)
    e21 = jnp.exp(t2 - t1)
    wa = pg / (1.0 + e21)
    wb = pg * e21 / (1.0 + e21)
    route = jnp.where(lane_i == 0, i1 - N_GROUPS,
                      jnp.where(lane_i == 1, i2 - N_GROUPS,
                                jnp.where(lane_i == 2, wa, jnp.where(lane_i == 3, wb, 0.0))))
    route_ref[...] = route


def _merge(x, y, g, bonus, yb, yc, zg, p, tm=256):
    n = x.shape[0]
    d = D_MODEL
    tok = lambda c: pl.BlockSpec((tm, c), lambda i: (i, 0))
    par = lambda r_, c_: pl.BlockSpec((r_, c_), lambda i: (0, 0))
    return pl.pallas_call(
        _merge_kernel,
        out_shape=[jax.ShapeDtypeStruct((n, d), f32), jax.ShapeDtypeStruct((n, d), f32),
                   jax.ShapeDtypeStruct((n, LANES), f32)],
        grid=(n // tm,),
        in_specs=[tok(d), tok(d), tok(d), tok(d), tok(256), tok(d), tok(G_COLS), par(1, d), par(1, d), par(1, G_COLS), par(d, d), par(256, d),
                  par(d, d), par(d, d), par(1, d), par(LANES, LANES), par(d, LANES), par(d, LANES),
                  par(1, LANES)],
        out_specs=[tok(d), tok(d), tok(LANES)],
        compiler_params=_cparams("parallel"),
        name="merge_router",
    )(x, y, g, bonus, yb, yc, zg, p["lnx_w"], p["lnx_b"], p["gate_b"], p["w_pa"], p["w_pb"],
      p["w_pc"], p["w_out"], p["norm_ffn"], p["bd"], p["rw_hi"], p["rw_lo"], p["rb"])


def _moe_kernel(blk_e_ref, nv_ref, nused_ref, tok_ref, tok_next_ref, dst_ref, h2_hbm, w1_ref, w3_ref, w2_ref,
                out_hbm, xbuf, ybuf, sem_in, sem_out):
    j = pl.program_id(0)
    rows = MOE_BLOCK
    nused = nused_ref[0]
    slot = j % 2

    def start_gather(idx_ref, buf):
        def body(i, c):
            pltpu.make_async_copy(h2_hbm.at[pl.ds(idx_ref[0, 0, i], 1), :],
                                  xbuf.at[buf, pl.ds(i, 1), :], sem_in.at[buf]).start()
            return c
        lax.fori_loop(0, rows, body, 0, unroll=8)

    def wait_scatter(count):
        bulk = pl.multiple_of((count // 8) * 8, 8)

        @pl.when(bulk > 0)
        def _():
            pltpu.make_async_copy(ybuf.at[pl.ds(0, bulk), :], out_hbm.at[pl.ds(0, bulk), :], sem_out).wait()

        def body(i, c):
            pltpu.make_async_copy(ybuf.at[pl.ds(0, 1), :], out_hbm.at[pl.ds(0, 1), :], sem_out).wait()
            return c
        lax.fori_loop(0, count - bulk, body, 0)

    @pl.when(j == 0)
    def _():
        start_gather(tok_ref, 0)

    def expert(between):
        pltpu.make_async_copy(h2_hbm.at[pl.ds(0, rows), :], xbuf.at[slot], sem_in.at[slot]).wait()
        x = xbuf[slot].astype(bf16)
        piece = 0
        hid = []
        for c in range(EXPERT_FF // MOE_PIECE):
            cols = slice(c * MOE_PIECE, (c + 1) * MOE_PIECE)
            h1 = jnp.dot(x, w1_ref[:, cols], preferred_element_type=f32)
            between(piece)
            h3 = jnp.dot(x, w3_ref[:, cols], preferred_element_type=f32)
            between(piece + 1)
            piece += 2
            hid.append((h1 * _sigmoid(h1) * h3).astype(bf16))
        hid = jnp.concatenate(hid, axis=1)
        ys = []
        for c in range(D_MODEL // MOE_PIECE):
            ys.append(jnp.dot(hid, w2_ref[:, c * MOE_PIECE:(c + 1) * MOE_PIECE], preferred_element_type=f32))
            between(piece)
            piece += 1
        return jnp.concatenate(ys, axis=1)

    n_pieces = 2 * (EXPERT_FF // MOE_PIECE) + D_MODEL // MOE_PIECE
    per_piece = rows // n_pieces

    def gather_next(k):
        for i in range(k * per_piece, (k + 1) * per_piece):
            pltpu.make_async_copy(h2_hbm.at[pl.ds(tok_next_ref[0, 0, i], 1), :],
                                  xbuf.at[1 - slot, pl.ds(i, 1), :], sem_in.at[1 - slot]).start()

    def finish(y):
        @pl.when(j > 0)
        def _():
            wait_scatter(nv_ref[j - 1])

        ybuf[...] = y
        nv = nv_ref[j]

        def body(i, c):
            pltpu.make_async_copy(ybuf.at[pl.ds(i, 1), :],
                                  out_hbm.at[pl.ds(dst_ref[0, 0, i], 1), :], sem_out).start()
            return c

        @pl.when(nv == rows)
        def _():
            lax.fori_loop(0, rows, body, 0, unroll=8)

        @pl.when(nv < rows)
        def _():
            lax.fori_loop(0, nv, body, 0)

        @pl.when(j == nused - 1)
        def _():
            wait_scatter(nv)

    @pl.when(j + 1 < nused)
    def _():
        finish(expert(gather_next))

    @pl.when(j + 1 == nused)
    def _():
        finish(expert(lambda k: None))


def _moe(h2, blk_e, nv, nused, tok, dst, w1, w3, w2, nblk, n_out_rows):
    d = D_MODEL
    idx_spec = lambda f: pl.BlockSpec((1, 1, MOE_BLOCK), f, memory_space=pltpu.SMEM)
    wspec = lambda r_, c_: pl.BlockSpec((None, r_, c_), lambda j, be, nv_, nu: (be[j], 0, 0))
    grid_spec = pltpu.PrefetchScalarGridSpec(
        num_scalar_prefetch=3,
        grid=(nblk,),
        in_specs=[idx_spec(lambda j, be, nv_, nu: (j, 0, 0)),
                  idx_spec(lambda j, be, nv_, nu: (jnp.minimum(j + 1, nblk - 1), 0, 0)),
                  idx_spec(lambda j, be, nv_, nu: (j, 0, 0)),
                  pl.BlockSpec(memory_space=pl.ANY),
                  wspec(d, EXPERT_FF), wspec(d, EXPERT_FF), wspec(EXPERT_FF, d)],
        out_specs=pl.BlockSpec(memory_space=pl.ANY),
        scratch_shapes=[pltpu.VMEM((2, MOE_BLOCK, d), f32), pltpu.VMEM((MOE_BLOCK, d), f32),
                        pltpu.SemaphoreType.DMA((2,)), pltpu.SemaphoreType.DMA(())],
    )
    return pl.pallas_call(
        _moe_kernel,
        out_shape=jax.ShapeDtypeStruct((n_out_rows, d), f32),
        grid_spec=grid_spec,
        compiler_params=_cparams("arbitrary"),
        name="moe_experts",
    )(blk_e, nv, nused, tok, tok, dst, h2, w1, w3, w2)


def _dispatch(route, n):
    nk = 2 * n
    flat_e = route[:, 0:2].astype(i32).reshape(-1)
    order = jnp.argsort(flat_e, stable=True).astype(i32)
    counts = jnp.sum(flat_e[:, None] == jnp.arange(N_EXPERTS, dtype=i32)[None, :], axis=0, dtype=i32)
    padded = (counts + MOE_BLOCK - 1) // MOE_BLOCK * MOE_BLOCK
    pend = jnp.cumsum(padded)
    pstart = pend - padded
    cstart = jnp.cumsum(counts) - counts
    nblk = nk // MOE_BLOCK + N_EXPERTS
    blk_start = jnp.arange(nblk, dtype=i32) * MOE_BLOCK
    blk_e = jnp.minimum(jnp.sum(pend[None, :] <= blk_start[:, None], axis=1, dtype=i32), N_EXPERTS - 1)
    off = blk_start - pstart[blk_e]
    nv = jnp.clip(counts[blk_e] - off, 0, MOE_BLOCK).astype(i32)
    within = jnp.arange(MOE_BLOCK, dtype=i32)[None, :]
    src = jnp.clip((cstart[blk_e] + off)[:, None] + within, 0, nk - 1)
    assign = order[src]
    valid = within < nv[:, None]
    tok = jnp.where(valid, assign // 2, 0)
    dst = jnp.where(valid, (assign % 2) * n + assign // 2, 0)
    nused = (pend[-1] // MOE_BLOCK).astype(i32).reshape(1)
    return (blk_e, nv, nused, tok.reshape(nblk, 1, MOE_BLOCK), dst.reshape(nblk, 1, MOE_BLOCK), nblk)


def _combine_kernel(emit_x, x1_ref, ya_ref, yb_ref, route_ref, g_ref, *outs):
    route = route_ref[...]
    x2 = x1_ref[...] + route[:, 2:3] * ya_ref[...] + route[:, 3:4] * yb_ref[...]
    hn = x2 * lax.rsqrt(jnp.mean(x2 * x2, axis=-1, keepdims=True) + RMS_EPS) * g_ref[...]
    if emit_x:
        outs[0][...] = x2
        outs[1][...] = hn.astype(outs[1].dtype)
    else:
        outs[0][...] = hn.astype(outs[0].dtype)


def _combine(x1, out2, route, g, emit_x, tm=512):
    n, d = x1.shape
    nt = n // tm
    tok = lambda c: pl.BlockSpec((tm, c), lambda i: (i, 0))
    if emit_x:
        out_shape = [jax.ShapeDtypeStruct((n, d), f32), jax.ShapeDtypeStruct((n, d), bf16)]
        out_specs = [tok(d), tok(d)]
    else:
        out_shape = [jax.ShapeDtypeStruct((n, d), f32)]
        out_specs = [tok(d)]
    return pl.pallas_call(
        functools.partial(_combine_kernel, emit_x),
        out_shape=out_shape,
        grid=(nt,),
        in_specs=[tok(d), tok(d), pl.BlockSpec((tm, d), lambda i: (i + nt, 0)), tok(LANES),
                  pl.BlockSpec((1, d), lambda i: (0, 0))],
        out_specs=out_specs,
        compiler_params=_cparams("parallel"),
        name="moe_combine",
    )(x1, out2, out2, route, g.reshape(1, d))


def _rope_table(T):
    pos = jnp.arange(T, dtype=f32)
    inv_freq = ROPE_THETA ** (-jnp.arange(0, HEAD, 2, dtype=f32) / HEAD)
    ang = pos[:, None] * inv_freq[None, :]
    cos, sin = jnp.cos(ang), jnp.sin(ang)
    cos_h = jnp.concatenate([cos, cos], axis=-1)
    sin_h = jnp.concatenate([-sin, sin], axis=-1)
    return jnp.concatenate([jnp.tile(cos_h, (1, 4)), jnp.tile(sin_h, (1, 4))], axis=-1)


def _pad_rows(w, rows, at=0):
    out = jnp.zeros((rows, w.shape[1]), w.dtype)
    return out.at[at:at + w.shape[0]].set(w)


def kernel(x, norm_mix, w_in, mu_a, w_decay0, w_decay2, a0, a2, g2, k_k, k_a, r_k, lnx_w, lnx_b, vres_mu, vres_w1, vres_w2, vres_b, sgu_ln_w, sgu_ln_b, sgu_ws, sgu_bs, gate_b, w_pa, w_pb, w_pc, w_out, norm_ffn, router_grp, router_grp_b, router_exp, router_exp_b, moe_w1, moe_w3, moe_w2, norm_final):
    bsz, T, d = x.shape
    n = bsz * T
    depth = w_in.shape[0]
    rope_tab = _rope_table(T)
    head_id = jnp.arange(LANES) // HEAD
    bd = (head_id[:, None] == head_id[None, :]).astype(bf16)
    row = lambda v: v.reshape(1, -1).astype(f32)

    xf = x.reshape(n, d)
    h = _rmsnorm(xf, norm_mix[0], bf16)
    v_first = None
    for l in range(depth):
        wl = w_in[l].astype(bf16)
        c0, c1, c2 = A_COLS, A_COLS + B_COLS, A_COLS + B_COLS + C_COLS
        za = _proj(h, wl[:, :c0], A_COLS // 2, "proj_a")
        zb = _proj(h, wl[:, c0:c1], B_COLS // 2, "proj_b")
        zc = _proj(h, wl[:, c1:c2], C_COLS // 2, "proj_c")
        zg = _proj(h, wl[:, c2:], G_COLS // 2, "proj_g")

        pa = dict(mu=row(mu_a[l]), w0=row(w_decay0[l]), w2=_pad_rows(w_decay2[l], LANES, 0).astype(bf16),
                  a0=row(a0[l]), a2=_pad_rows(a2[l], LANES, 64).astype(bf16), g2=g2[l].astype(bf16),
                  k_k=row(k_k[l]), k_a=row(k_a[l]), r_k=row(r_k[l]), bd=bd)
        vres = None
        if l > 0:
            w1p = jnp.zeros((d, LANES), f32).at[:, :vres_w1.shape[2]].set(vres_w1[l - 1]).astype(bf16)
            mup = jnp.zeros((1, LANES), f32).at[0, :vres_mu.shape[1]].set(vres_mu[l - 1])
            vres = dict(h=h, vfirst=v_first, w1=w1p, mu=mup,
                        w2=_pad_rows(vres_w2[l - 1], LANES, 0).astype(bf16), b=row(vres_b[l - 1]))
        rt, at, v, bT, kT, pcT, g, bonus = _wkv_prep(za, bsz, T, pa, vres)
        if l == 0:
            v_first = v
        y = _wkv_scan(rt, at, v, bT, kT, pcT, bsz, T).reshape(n, d)

        yb = _attention(zb, rope_tab, bsz, T)

        bs_b = jnp.broadcast_to(sgu_bs[l][:, :, None], (SGU_GROUPS, SGU_CHUNK, LANES)).astype(f32)
        yc = _sgu(zc, row(sgu_ln_w[l]), row(sgu_ln_b[l]), sgu_ws[l], bs_b)

        rw = jnp.zeros((d, LANES), f32).at[:, :N_GROUPS].set(router_grp[l]) \
            .at[:, N_GROUPS:N_GROUPS + N_EXPERTS].set(router_exp[l])
        rw_hi = rw.astype(bf16)
        rw_lo = (rw - rw_hi.astype(f32)).astype(bf16)
        rb = jnp.zeros((1, LANES), f32).at[0, :N_GROUPS].set(router_grp_b[l]) \
            .at[0, N_GROUPS:N_GROUPS + N_EXPERTS].set(router_exp_b[l])
        pm = dict(lnx_w=row(lnx_w[l]), lnx_b=row(lnx_b[l]), gate_b=row(gate_b[l]), w_pa=w_pa[l].astype(bf16),
                  w_pb=w_pb[l].astype(bf16), w_pc=w_pc[l].astype(bf16), w_out=w_out[l].astype(bf16),
                  norm_ffn=row(norm_ffn[l]), bd=bd, rw_hi=rw_hi, rw_lo=rw_lo, rb=rb)
        x1, h2, route = _merge(xf, y, g.reshape(n, d), bonus.reshape(n, d), yb, yc, zg, pm)

        blk_e, nv, nused, tok, dst, nblk = _dispatch(route, n)
        out2 = _moe(h2, blk_e, nv, nused, tok, dst, moe_w1[l].astype(bf16), moe_w3[l].astype(bf16),
                    moe_w2[l].astype(bf16), nblk, 2 * n)
        if l + 1 < depth:
            xf, h = _combine(x1, out2, route, norm_mix[l + 1], True)
        else:
            (out,) = _combine(x1, out2, route, norm_final, False)
    return out.reshape(bsz, T, d)
```

```python
import functools
import math

import jax
import jax.numpy as jnp
from jax import lax
from jax.experimental import pallas as pl
from jax.experimental.pallas import tpu as pltpu

f32 = jnp.float32
bf16 = jnp.bfloat16
i32 = jnp.int32

D_MODEL = 1024
HEAD = 64
RMS_EPS = 1e-6
GN_EPS = 64e-5
LN_EPS = 1e-5
ROPE_THETA = 10000.0
A_WIDTH = D_MODEL
A_COLS = 3 * A_WIDTH + 64 + 64 + 128
B_COLS = 3 * 768
C_COLS = 2 * D_MODEL
G_COLS = 3 * D_MODEL
DILATIONS = (1, 4, 16)
ATT_BLOCK = 128
ATT_SPAN = 128
ATT_TILE = ATT_BLOCK * max(DILATIONS)
ATT_UNROLL = 2
SGU_CHUNK = 128
SGU_GROUPS = 8
N_GROUPS = 4
EXPERTS_PER_GROUP = 8
N_EXPERTS = N_GROUPS * EXPERTS_PER_GROUP
EXPERT_FF = D_MODEL // 2
MOE_BLOCK = 256
MOE_PIECE = 256
WKV_CHUNK = 128
SCAN_GROUP = 8
LANES = 128
VMEM_LIMIT = 48 * 1024 * 1024


def _cparams(*sem):
    return pltpu.CompilerParams(dimension_semantics=sem, vmem_limit_bytes=VMEM_LIMIT)


def _bdot(a, b):
    return jnp.dot(a.astype(bf16), b.astype(bf16), preferred_element_type=f32)


def _dot_hilo(a, b):
    hi = a.astype(bf16)
    lo = (a - hi.astype(f32)).astype(bf16)
    return (jnp.dot(hi, b, preferred_element_type=f32)
            + jnp.dot(lo, b, preferred_element_type=f32))


def _head_sum(x, bd):
    parts = [_dot_hilo(x[:, p * LANES:(p + 1) * LANES], bd) for p in range(x.shape[1] // LANES)]
    return jnp.concatenate(parts, axis=1)


def _sigmoid(x):
    return 0.5 + 0.5 * jnp.tanh(0.5 * x)


def _rmsnorm_kernel(x_ref, g_ref, o_ref):
    x = x_ref[...]
    y = x * lax.rsqrt(jnp.mean(x * x, axis=-1, keepdims=True) + RMS_EPS)
    o_ref[...] = (y * g_ref[...]).astype(o_ref.dtype)


def _rmsnorm(x, g, out_dtype, tm=1024):
    n, d = x.shape
    return pl.pallas_call(
        _rmsnorm_kernel,
        out_shape=jax.ShapeDtypeStruct((n, d), out_dtype),
        grid=(n // tm,),
        in_specs=[pl.BlockSpec((tm, d), lambda i: (i, 0)), pl.BlockSpec((1, d), lambda i: (0, 0))],
        out_specs=pl.BlockSpec((tm, d), lambda i: (i, 0)),
        compiler_params=_cparams("parallel"),
        name="rmsnorm",
    )(x, g.reshape(1, d))


def _matmul_kernel(h_ref, w_ref, o_ref):
    o_ref[...] = jnp.dot(h_ref[...], w_ref[...], preferred_element_type=f32).astype(o_ref.dtype)


def _proj(h, w, tn, name, tm=1024):
    n, k = h.shape
    m = w.shape[1]
    return pl.pallas_call(
        _matmul_kernel,
        out_shape=jax.ShapeDtypeStruct((n, m), bf16),
        grid=(m // tn, n // tm),
        in_specs=[pl.BlockSpec((tm, k), lambda j, i: (i, 0)), pl.BlockSpec((k, tn), lambda j, i: (0, j))],
        out_specs=pl.BlockSpec((tm, tn), lambda j, i: (i, j)),
        compiler_params=_cparams("parallel", "parallel"),
        name=name,
    )(h, w)


def _wkv_prep_kernel(has_vres, tm, *refs):
    if has_vres:
        (h_ref, wa_ref, vfirst_ref, mu_ref, w0_ref, w2_ref, a0_ref, a2_ref, g2_ref, kk_ref, ka_ref,
         rk_ref, bd_ref, shift_ref, tri_ref, vw1_ref, vmu_ref, vw2_ref, vb_ref,
         rt_ref, at_ref, v_ref, bT_ref, kT_ref, pcT_ref, g_ref, bonus_ref, carry, carry_v) = refs
    else:
        (h_ref, wa_ref, mu_ref, w0_ref, w2_ref, a0_ref, a2_ref, g2_ref, kk_ref, ka_ref, rk_ref, bd_ref,
         shift_ref, tri_ref,
         rt_ref, at_ref, v_ref, bT_ref, kT_ref, pcT_ref, g_ref, bonus_ref, carry) = refs
    t = pl.program_id(1)

    @pl.when(t == 0)
    def _():
        carry[...] = jnp.zeros_like(carry)
        if has_vres:
            carry_v[...] = jnp.zeros_like(carry_v)

    row = lax.broadcasted_iota(i32, (tm, 1), 0)
    zb = jnp.dot(h_ref[...], wa_ref[...], preferred_element_type=f32).astype(bf16)
    z = zb.astype(f32)
    shifted = jnp.dot(shift_ref[...], zb, preferred_element_type=f32)
    prev = jnp.where(row == 0, carry[0:1, :], shifted)
    carry[0:1, :] = z[tm - 1:tm, :]
    zs = z + (prev - z) * mu_ref[...]
    w = A_WIDTH
    r, k, v = zs[:, 0:w], zs[:, w:2 * w], zs[:, 2 * w:3 * w]
    lo = zs[:, 3 * w:3 * w + LANES]
    glo = zs[:, 3 * w + LANES:3 * w + 2 * LANES]

    if has_vres:
        hv = jnp.dot(h_ref[...], vw1_ref[...], preferred_element_type=f32)
        prev_v = jnp.where(row == 0, carry_v[0:1, :], pltpu.roll(hv, 1, axis=0))
        carry_v[0:1, :] = hv[tm - 1:tm, :]
        vlo = hv + (prev_v - hv) * vmu_ref[...]
        sg = _sigmoid(vb_ref[...] + _bdot(vlo, vw2_ref[...]))
        v = v + (vfirst_ref[...].astype(f32) - v) * sg

    wl = w0_ref[...] + _bdot(jnp.tanh(lo), w2_ref[...])
    yy = -wl
    softplus = jnp.maximum(yy, 0.0) + jnp.log(1.0 + jnp.exp(-jnp.abs(yy)))
    ld = -jnp.exp(-softplus - 0.5)
    a = _sigmoid(a0_ref[...] + _bdot(lo, a2_ref[...]))
    g = _bdot(_sigmoid(glo), g2_ref[...])
    bd = bd_ref[...]
    kk = k * kk_ref[...]
    kk = kk * lax.rsqrt(jnp.maximum(_head_sum(kk * kk, bd), 1e-24))
    kp = k * (1.0 + (a - 1.0) * ka_ref[...])
    bonus = _head_sum(r * kp * rk_ref[...], bd) * v

    tri = tri_ref[...]
    ld_hi = ld.astype(bf16)
    res = ld - ld_hi.astype(f32)
    ld_mid = res.astype(bf16)
    ld_lo = (res - ld_mid.astype(f32)).astype(bf16)
    cum = (jnp.dot(tri, ld_hi, preferred_element_type=f32) + jnp.dot(tri, ld_mid, preferred_element_type=f32)
           + jnp.dot(tri, ld_lo, preferred_element_type=f32))
    p_inc = jnp.exp(cum)
    p_inv = jnp.exp(-cum)
    ends = [cum[(c + 1) * WKV_CHUNK - 1:(c + 1) * WKV_CHUNK, :] for c in range(tm // WKV_CHUNK)]
    cum_end = ends[0]
    for c in range(1, tm // WKV_CHUNK):
        cum_end = jnp.where(row >= c * WKV_CHUNK, ends[c], cum_end)
    cum_end = jnp.broadcast_to(cum_end, (tm, w))

    rt_ref[...] = (r * p_inc).astype(bf16)
    at_ref[...] = (-kk * jnp.exp(cum - ld)).astype(bf16)
    v_ref[...] = v.astype(bf16)
    bT_ref[...] = (kk * a * p_inv).T.astype(bf16)
    kT_ref[...] = (kp * p_inv).T.astype(bf16)
    pcT_ref[...] = jnp.exp(cum_end).T
    g_ref[...] = g.astype(bf16)
    bonus_ref[...] = bonus.astype(bf16)


def _wkv_prep(h, wa, bsz, T, p, vres, tm=256):
    has_vres = vres is not None
    w = A_WIDTH
    tok = lambda cols: pl.BlockSpec((None, tm, cols), lambda b, t: (b, t, 0))
    par = lambda r_, c_: pl.BlockSpec((r_, c_), lambda b, t: (0, 0))
    ins = [h.reshape(bsz, T, D_MODEL), wa]
    specs = [tok(D_MODEL), pl.BlockSpec((D_MODEL, A_COLS), lambda b, t: (0, 0), pipeline_mode=pl.Buffered(1))]
    if has_vres:
        ins += [vres["vfirst"]]
        specs += [tok(w)]
    ti = jnp.arange(tm)
    shift = (ti[:, None] == ti[None, :] + 1).astype(bf16)
    tri = ((ti[:, None] >= ti[None, :]) & (ti[:, None] // WKV_CHUNK == ti[None, :] // WKV_CHUNK)).astype(bf16)
    ins += [p["mu"], p["w0"], p["w2"], p["a0"], p["a2"], p["g2"], p["k_k"], p["k_a"], p["r_k"], p["bd"],
            shift, tri]
    specs += [par(1, A_COLS), par(1, w), par(LANES, w), par(1, w), par(LANES, w), par(LANES, w),
              par(1, w), par(1, w), par(1, w), par(LANES, LANES), par(tm, tm), par(tm, tm)]
    if has_vres:
        ins += [vres["w1"], vres["mu"], vres["w2"], vres["b"]]
        specs += [par(D_MODEL, LANES), par(1, LANES), par(LANES, w), par(1, w)]
    tr = pl.BlockSpec((None, w, tm), lambda b, t: (b, 0, t))
    out_shape = [jax.ShapeDtypeStruct((bsz, T, w), bf16)] * 3 + \
                [jax.ShapeDtypeStruct((bsz, w, T), bf16)] * 2 + \
                [jax.ShapeDtypeStruct((bsz, w, T), f32)] + \
                [jax.ShapeDtypeStruct((bsz, T, w), bf16)] * 2
    out_specs = [tok(w)] * 3 + [tr] * 3 + [tok(w)] * 2
    scratch = [pltpu.VMEM((8, A_COLS), f32)]
    if has_vres:
        scratch.append(pltpu.VMEM((8, LANES), f32))
    return pl.pallas_call(
        functools.partial(_wkv_prep_kernel, has_vres, tm),
        out_shape=out_shape,
        grid=(bsz, T // tm),
        in_specs=specs,
        out_specs=out_specs,
        scratch_shapes=scratch,
        compiler_params=_cparams("parallel", "arbitrary"),
        name="wkv_prep",
    )(*ins)


def _wkv_scan_kernel(rt_ref, at_ref, v_ref, bT_ref, kT_ref, pcT_ref, y_ref, st_ref):
    C = WKV_CHUNK

    @pl.when(pl.program_id(1) == 0)
    def _():
        st_ref[...] = jnp.zeros_like(st_ref)

    head0 = lax.broadcasted_iota(i32, (1, LANES), 1) < HEAD
    key0 = lax.broadcasted_iota(i32, (LANES, 1), 0) < HEAD
    same_head = head0 == key0
    ri = lax.broadcasted_iota(i32, (C, C), 0)
    ci = lax.broadcasted_iota(i32, (C, C), 1)
    strict = ri > ci
    lower2 = jnp.concatenate([ri >= ci, ri >= ci], axis=1)
    eye = (ri == ci).astype(f32)

    fdot = lambda a, b: jnp.dot(a, b, preferred_element_type=f32)
    hms = (head0, jnp.logical_not(head0))

    def pair_group(gi, carry_):
        pis = [gi * SCAN_GROUP + q for q in range(SCAN_GROUP)]
        c0s = [pl.multiple_of(pi * LANES, LANES) for pi in pis]
        ars = [jnp.concatenate([at_ref[:, pl.ds(c0, LANES)], rt_ref[:, pl.ds(c0, LANES)]], axis=0) for c0 in c0s]
        vs = [v_ref[:, pl.ds(c0, LANES)] for c0 in c0s]
        bks = [jnp.concatenate([bT_ref[pl.ds(c0, LANES), :], kT_ref[pl.ds(c0, LANES), :]], axis=1) for c0 in c0s]
        sts = [st_ref[pi] for pi in pis]
        xss = [fdot(ar, st.astype(bf16)) for ar, st in zip(ars, sts)]
        heads = [(q, hm) for q in range(SCAN_GROUP) for hm in hms]
        grams = [fdot(jnp.where(hm, ars[q], 0), bks[q]) for q, hm in heads]
        npows = [jnp.where(strict, gm[0:C, 0:C], 0.0) for gm in grams]
        a_aks = [jnp.where(strict, gm[0:C, C:2 * C], 0.0).astype(bf16) for gm in grams]
        a_rs = [jnp.where(lower2, gm[C:2 * C, :], 0.0).astype(bf16) for gm in grams]
        xins = [xss[q][0:C] + fdot(a_ak, vs[q]) for (q, _), a_ak in zip(heads, a_aks)]
        winvs = [eye + n for n in npows]
        s = 2
        while s < C:
            npows = [_bdot(n, n) for n in npows]
            winvs = [w + _bdot(w, n) for w, n in zip(winvs, npows)]
            s *= 2
        us = [_bdot(w, x) for w, x in zip(winvs, xins)]
        for q in range(SCAN_GROUP):
            uv = jnp.concatenate([jnp.where(head0, us[2 * q], us[2 * q + 1]).astype(bf16), vs[q]], axis=0)
            y_ref[:, pl.ds(c0s[q], LANES)] = xss[q][C:2 * C] + jnp.where(
                head0, fdot(a_rs[2 * q], uv), fdot(a_rs[2 * q + 1], uv))
            pc = pcT_ref[pl.ds(c0s[q], LANES), :]
            bkp = (bks[q].astype(f32) * jnp.concatenate([pc, pc], axis=1)).astype(bf16)
            st_ref[pis[q]] = sts[q] * pc + jnp.where(same_head, fdot(bkp, uv), 0.0)
        return carry_

    lax.fori_loop(0, A_WIDTH // LANES // SCAN_GROUP, pair_group, 0)


def _wkv_scan(rt, at, v, bT, kT, pcT, bsz, T):
    C = WKV_CHUNK
    w = A_WIDTH
    tok = pl.BlockSpec((None, C, w), lambda b, t: (b, t, 0))
    tr = pl.BlockSpec((None, w, C), lambda b, t: (b, 0, t))
    return pl.pallas_call(
        _wkv_scan_kernel,
        out_shape=jax.ShapeDtypeStruct((bsz, T, w), f32),
        grid=(bsz, T // C),
        in_specs=[tok, tok, tok, tr, tr, tr],
        out_specs=tok,
        scratch_shapes=[pltpu.VMEM((w // LANES, LANES, LANES), f32)],
        compiler_params=_cparams("parallel", "arbitrary"),
        name="wkv_scan",
    )(rt, at, v, bT, kT, pcT)


def _attn_kernel(q_ref, k_ref, v_ref, cos_ref, sin_ref, o_ref,
                 qs, kall, vall, ktails, vtails, m_acc, num_acc, den_acc):
    t = pl.program_id(1)
    g = pl.program_id(2)
    TT = ATT_TILE
    blk = ATT_BLOCK
    width = q_ref.shape[-1]
    lane = lax.broadcasted_iota(i32, (1, width), 1)
    first_half = (lane % HEAD) < (HEAD // 2)
    cos = cos_ref[...]
    sin = sin_ref[...]

    def rope(z):
        partner = jnp.where(first_half, pltpu.roll(z, width - HEAD // 2, axis=1),
                            pltpu.roll(z, HEAD // 2, axis=1))
        return z * cos + partner * sin

    halves = width // LANES

    def put(ref, r0, val):
        for c in range(halves):
            ref[c, r0:r0 + val.shape[0], :] = val[:, c * LANES:(c + 1) * LANES]

    def take(ref, r0, n):
        return jnp.concatenate([ref[c, r0:r0 + n, :] for c in range(halves)], axis=1)

    put(qs, 0, rope(q_ref[...].astype(f32)) * (HEAD ** -0.5))
    put(kall, TT, rope(k_ref[...].astype(f32)))
    put(vall, TT, v_ref[...].astype(f32))

    qi = lax.broadcasted_iota(i32, (blk, 2 * blk), 0)
    kj = lax.broadcasted_iota(i32, (blk, 2 * blk), 1)
    dist = qi + blk - kj
    band = (dist >= 0) & (dist <= ATT_SPAN)
    cur_keys = kj >= blk
    head_masks = [(lane // HEAD) == h for h in range(width // HEAD)]

    def group(gi, d):
        tail = blk * d
        ktail, vtail = ktails[gi], vtails[gi]

        @pl.when(t == 0)
        def _():
            put(kall, TT - tail, jnp.zeros((tail, width), f32))
            put(vall, TT - tail, jnp.zeros((tail, width), f32))

        @pl.when(t > 0)
        def _():
            put(kall, TT - tail, ktail[...])
            put(vall, TT - tail, vtail[...])

        def unit_group(ug, carry_):
            units = []
            for j in range(ATT_UNROLL):
                u = ug * ATT_UNROLL + j
                nl = u // d
                start = u % d + tail * nl
                rows = functools.partial(lambda s0, base: pl.ds(base + s0, blk, stride=d), start)
                ld = functools.partial(
                    lambda rw, ref, base: jnp.concatenate([ref[c, rw(base), :] for c in range(halves)], axis=1), rows)
                q = ld(qs, 0)
                kcat = jnp.concatenate([ld(kall, TT - tail), ld(kall, TT)], axis=0).astype(bf16)
                vcat = jnp.concatenate([ld(vall, TT - tail), ld(vall, TT)], axis=0).astype(bf16)
                mask = band & (cur_keys | jnp.logical_or(t > 0, nl > 0))
                units.append((rows, ld, q, kcat, vcat, mask))
            chains = [(un, hm) for un in units for hm in head_masks]
            ss = [lax.dot_general(jnp.where(hm, un[2], 0.0).astype(bf16), un[3], (((1,), (1,)), ((), ())),
                                  preferred_element_type=f32) for un, hm in chains]
            ss = [jnp.where(un[5], s, -jnp.inf) for (un, _), s in zip(chains, ss)]
            ms = [jnp.max(s, axis=-1, keepdims=True) for s in ss]
            es = [jnp.exp(s - m) for s, m in zip(ss, ms)]
            dens = [jnp.sum(e, axis=-1, keepdims=True) for e in es]
            pvs = [jnp.dot(e.astype(bf16), un[4], preferred_element_type=f32) for (un, _), e in zip(chains, es)]
            nh = len(head_masks)
            for j, (rows, ld, _, _, _, _) in enumerate(units):
                num, m_full, den_full = pvs[j * nh], ms[j * nh], dens[j * nh]
                for h in range(1, nh):
                    hm = head_masks[h]
                    num = jnp.where(hm, pvs[j * nh + h], num)
                    m_full = jnp.where(hm, ms[j * nh + h], m_full)
                    den_full = jnp.where(hm, dens[j * nh + h], den_full)

                def st(ref, val, rows=rows):
                    for c in range(halves):
                        ref[c, rows(0), :] = val[:, c * LANES:(c + 1) * LANES]

                if gi == 0:
                    st(m_acc, m_full)
                    st(num_acc, num)
                    st(den_acc, den_full)
                else:
                    m_old = ld(m_acc, 0)
                    m_new = jnp.maximum(m_old, m_full)
                    w_old = jnp.exp(m_old - m_new)
                    w_new = jnp.exp(m_full - m_new)
                    st(m_acc, m_new)
                    st(num_acc, ld(num_acc, 0) * w_old + num * w_new)
                    st(den_acc, ld(den_acc, 0) * w_old + den_full * w_new)
            return carry_

        lax.fori_loop(0, TT // blk // ATT_UNROLL, unit_group, 0)
        ktail[...] = take(kall, 2 * TT - tail, tail)
        vtail[...] = take(vall, 2 * TT - tail, tail)

    for gi, d in enumerate(DILATIONS):
        pl.when(g == gi)(functools.partial(group, gi, d))

    @pl.when(g == len(DILATIONS) - 1)
    def _():
        o_ref[...] = (take(num_acc, 0, TT) / take(den_acc, 0, TT)).astype(o_ref.dtype)


def _attention(zb, rope_tab, bsz, T):
    TT = ATT_TILE
    width = 4 * HEAD
    zv = zb.reshape(bsz, T, B_COLS)
    col = lambda off: pl.BlockSpec((None, TT, width), lambda b, t, g: (b, t, off + g))
    tab = lambda off: pl.BlockSpec((TT, width), lambda b, t, g: (t, off))
    tails = [pltpu.VMEM((ATT_BLOCK * d, width), f32) for d in DILATIONS]
    tiled = lambda rows: pltpu.VMEM((width // LANES, rows, LANES), f32)
    out = pl.pallas_call(
        _attn_kernel,
        out_shape=jax.ShapeDtypeStruct((bsz, T, width), bf16),
        grid=(bsz, T // TT, len(DILATIONS)),
        in_specs=[col(0), col(3), col(6), tab(0), tab(1)],
        out_specs=pl.BlockSpec((None, TT, width), lambda b, t, g: (b, t, 0)),
        scratch_shapes=[tiled(TT), tiled(2 * TT), tiled(2 * TT), tails, list(tails),
                        tiled(TT), tiled(TT), tiled(TT)],
        compiler_params=_cparams("parallel", "arbitrary", "arbitrary"),
        name="dilated_attn",
    )(zv, zv, zv, rope_tab, rope_tab)
    return out.reshape(bsz * T, width)


def _sgu_kernel(tm, h_ref, wc_ref, lnw_ref, lnb_ref, ws_ref, bs_ref, o_ref):
    ch = SGU_CHUNK
    z = jnp.dot(h_ref[...], wc_ref[...], preferred_element_type=f32).astype(bf16).astype(f32)
    ge = 0.5 * z * (1.0 + lax.erf(z * (2.0 ** -0.5)))
    u = ge[:, :D_MODEL]
    v = ge[:, D_MODEL:]
    mean = jnp.mean(v, axis=-1, keepdims=True)
    var = jnp.mean(jnp.square(v - mean), axis=-1, keepdims=True)
    vn = ((v - mean) * lax.rsqrt(var + LN_EPS) * lnw_ref[...] + lnb_ref[...]).astype(bf16)
    ri = lax.broadcasted_iota(i32, (ch, ch), 0)
    ci = lax.broadcasted_iota(i32, (ch, ch), 1)
    causal = ri >= ci
    for g in range(SGU_GROUPS):
        wg = jnp.where(causal, ws_ref[g], 0.0).astype(bf16)
        bias = bs_ref[g]
        for c in range(tm // ch):
            rows = slice(c * ch, (c + 1) * ch)
            cols = slice(g * LANES, (g + 1) * LANES)
            mixed = jnp.dot(wg, vn[rows, cols], preferred_element_type=f32) + bias
            o_ref[rows, cols] = (u[rows, cols] * mixed).astype(o_ref.dtype)


def _sgu(h, wc, lnw, lnb, ws, bs_b, tm=512):
    n = h.shape[0]
    return pl.pallas_call(
        functools.partial(_sgu_kernel, tm),
        out_shape=jax.ShapeDtypeStruct((n, D_MODEL), bf16),
        grid=(n // tm,),
        in_specs=[pl.BlockSpec((tm, D_MODEL), lambda i: (i, 0)),
                  pl.BlockSpec((D_MODEL, C_COLS), lambda i: (0, 0), pipeline_mode=pl.Buffered(1)),
                  pl.BlockSpec((1, D_MODEL), lambda i: (0, 0)),
                  pl.BlockSpec((1, D_MODEL), lambda i: (0, 0)),
                  pl.BlockSpec((SGU_GROUPS, SGU_CHUNK, SGU_CHUNK), lambda i: (0, 0, 0)),
                  pl.BlockSpec((SGU_GROUPS, SGU_CHUNK, LANES), lambda i: (0, 0, 0))],
        out_specs=pl.BlockSpec((tm, D_MODEL), lambda i: (i, 0)),
        compiler_params=_cparams("parallel"),
        name="sgu",
    )(h, wc, lnw, lnb, ws, bs_b)


def _merge_kernel(x_ref, y_ref, g_ref, bonus_ref, yb_ref, yc_ref, h_ref, wg_ref, lnxw_ref, lnxb_ref, gb_ref,
                  wpa_ref, wpb_ref, wpc_ref, wout_ref, nf_ref, bd_ref, rwh_ref, rwl_ref, rb_ref, x1_ref, h2_ref, route_ref):
    bd = bd_ref[...]
    y = y_ref[...]
    mean = _head_sum(y, bd) * (1.0 / HEAD)
    dy = y - mean
    var = _head_sum(dy * dy, bd) * (1.0 / HEAD)
    yn = dy * lax.rsqrt(var + GN_EPS) * lnxw_ref[...] + lnxb_ref[...] + bonus_ref[...].astype(f32)
    ya = (yn * g_ref[...].astype(f32)).astype(bf16)

    zg = jnp.dot(h_ref[...], wg_ref[...], preferred_element_type=f32).astype(bf16)
    gates = _sigmoid(zg.astype(f32) + gb_ref[...])
    d = D_MODEL
    merged = (gates[:, 0:d] * jnp.dot(ya, wpa_ref[...], preferred_element_type=f32)
              + gates[:, d:2 * d] * jnp.dot(yb_ref[...], wpb_ref[...], preferred_element_type=f32)
              + gates[:, 2 * d:3 * d] * jnp.dot(yc_ref[...], wpc_ref[...], preferred_element_type=f32))
    x1 = x_ref[...] + _bdot(merged, wout_ref[...])
    x1_ref[...] = x1
    hn = x1 * lax.rsqrt(jnp.mean(x1 * x1, axis=-1, keepdims=True) + RMS_EPS) * nf_ref[...]
    h2_ref[...] = hn

    hh = hn.astype(bf16)
    hl = (hn - hh.astype(f32)).astype(bf16)
    logits = (jnp.dot(hh, rwh_ref[...], preferred_element_type=f32)
              + jnp.dot(hl, rwh_ref[...], preferred_element_type=f32)
              + jnp.dot(hh, rwl_ref[...], preferred_element_type=f32)) + rb_ref[...]
    lane_i = lax.broadcasted_iota(i32, logits.shape, 1)
    lane = lane_i.astype(f32)
    lane_grp = ((lane_i - N_GROUPS) // EXPERTS_PER_GROUP).astype(f32)
    neg = jnp.float32(-1e30)
    big = jnp.float32(1e9)
    is_g = lane_i < N_GROUPS
    gl = jnp.where(is_g, logits, neg)
    gmax = jnp.max(gl, axis=-1, keepdims=True)
    gsel = jnp.min(jnp.where(gl == gmax, lane, big), axis=-1, keepdims=True)
    gsum = jnp.sum(jnp.where(is_g, jnp.exp(gl - gmax), 0.0), axis=-1, keepdims=True)
    pg = 1.0 / gsum
    emask = (lane_i >= N_GROUPS) & (lane_i < N_GROUPS + N_EXPERTS) & (lane_grp == gsel)
    el = jnp.where(emask, logits, neg)
    t1 = jnp.max(el, axis=-1, keepdims=True)
    i1 = jnp.min(jnp.where(el == t1, lane, big), axis=-1, keepdims=True)
    el2 = jnp.where(lane == i1, neg, el)
    t2 = jnp.max(el2, axis=-1, keepdims=True)
    i2 = jnp.min(jnp.where(el2 == t2, lane, big), axis=-1, keepdims=True)
    e21 = jnp.exp(t2 - t1)
    wa = pg / (1.0 + e21)
    wb = pg * e21 / (1.0 + e21)
    route = jnp.where(lane_i == 0, i1 - N_GROUPS,
                      jnp.where(lane_i == 1, i2 - N_GROUPS,
                                jnp.where(lane_i == 2, wa, jnp.where(lane_i == 3, wb, 0.0))))
    route_ref[...] = route


def _merge(x, y, g, bonus, yb, yc, h, wg, p, tm=256):
    n = x.shape[0]
    d = D_MODEL
    tok = lambda c: pl.BlockSpec((tm, c), lambda i: (i, 0))
    par = lambda r_, c_: pl.BlockSpec((r_, c_), lambda i: (0, 0))
    return pl.pallas_call(
        _merge_kernel,
        out_shape=[jax.ShapeDtypeStruct((n, d), f32), jax.ShapeDtypeStruct((n, d), f32),
                   jax.ShapeDtypeStruct((n, LANES), f32)],
        grid=(n // tm,),
        in_specs=[tok(d), tok(d), tok(d), tok(d), tok(256), tok(d), tok(d),
                  pl.BlockSpec((d, G_COLS), lambda i: (0, 0), pipeline_mode=pl.Buffered(1)),
                  par(1, d), par(1, d), par(1, G_COLS), par(d, d), par(256, d),
                  par(d, d), par(d, d), par(1, d), par(LANES, LANES), par(d, LANES), par(d, LANES),
                  par(1, LANES)],
        out_specs=[tok(d), tok(d), tok(LANES)],
        compiler_params=_cparams("parallel"),
        name="merge_router",
    )(x, y, g, bonus, yb, yc, h, wg, p["lnx_w"], p["lnx_b"], p["gate_b"], p["w_pa"], p["w_pb"],
      p["w_pc"], p["w_out"], p["norm_ffn"], p["bd"], p["rw_hi"], p["rw_lo"], p["rb"])


def _moe_kernel(blk_e_ref, nv_ref, nused_ref, tok_ref, tok_next_ref, dst_ref, h2_hbm, w1_ref, w3_ref, w2_ref,
                out_hbm, xbuf, ybuf, sem_in, sem_out):
    j = pl.program_id(0)
    rows = MOE_BLOCK
    nused = nused_ref[0]
    slot = j % 2

    def start_gather(idx_ref, buf):
        def body(i, c):
            pltpu.make_async_copy(h2_hbm.at[pl.ds(idx_ref[0, 0, i], 1), :],
                                  xbuf.at[buf, pl.ds(i, 1), :], sem_in.at[buf]).start()
            return c
        lax.fori_loop(0, rows, body, 0, unroll=8)

    def wait_scatter(count):
        bulk = pl.multiple_of((count // 8) * 8, 8)

        @pl.when(bulk > 0)
        def _():
            pltpu.make_async_copy(ybuf.at[pl.ds(0, bulk), :], out_hbm.at[pl.ds(0, bulk), :], sem_out).wait()

        def body(i, c):
            pltpu.make_async_copy(ybuf.at[pl.ds(0, 1), :], out_hbm.at[pl.ds(0, 1), :], sem_out).wait()
            return c
        lax.fori_loop(0, count - bulk, body, 0)

    @pl.when(j == 0)
    def _():
        start_gather(tok_ref, 0)

    def expert(between):
        pltpu.make_async_copy(h2_hbm.at[pl.ds(0, rows), :], xbuf.at[slot], sem_in.at[slot]).wait()
        x = xbuf[slot].astype(bf16)
        piece = 0
        hid = []
        for c in range(EXPERT_FF // MOE_PIECE):
            cols = slice(c * MOE_PIECE, (c + 1) * MOE_PIECE)
            h1 = jnp.dot(x, w1_ref[:, cols], preferred_element_type=f32)
            between(piece)
            h3 = jnp.dot(x, w3_ref[:, cols], preferred_element_type=f32)
            between(piece + 1)
            piece += 2
            hid.append((h1 * _sigmoid(h1) * h3).astype(bf16))
        hid = jnp.concatenate(hid, axis=1)
        ys = []
        for c in range(D_MODEL // MOE_PIECE):
            ys.append(jnp.dot(hid, w2_ref[:, c * MOE_PIECE:(c + 1) * MOE_PIECE], preferred_element_type=f32))
            between(piece)
            piece += 1
        return jnp.concatenate(ys, axis=1)

    n_pieces = 2 * (EXPERT_FF // MOE_PIECE) + D_MODEL // MOE_PIECE
    per_piece = rows // n_pieces

    def gather_next(k):
        for i in range(k * per_piece, (k + 1) * per_piece):
            pltpu.make_async_copy(h2_hbm.at[pl.ds(tok_next_ref[0, 0, i], 1), :],
                                  xbuf.at[1 - slot, pl.ds(i, 1), :], sem_in.at[1 - slot]).start()

    def finish(y):
        @pl.when(j > 0)
        def _():
            wait_scatter(nv_ref[j - 1])

        ybuf[...] = y
        nv = nv_ref[j]

        def body(i, c):
            pltpu.make_async_copy(ybuf.at[pl.ds(i, 1), :],
                                  out_hbm.at[pl.ds(dst_ref[0, 0, i], 1), :], sem_out).start()
            return c

        @pl.when(nv == rows)
        def _():
            lax.fori_loop(0, rows, body, 0, unroll=8)

        @pl.when(nv < rows)
        def _():
            lax.fori_loop(0, nv, body, 0)

        @pl.when(j == nused - 1)
        def _():
            wait_scatter(nv)

    @pl.when(j + 1 < nused)
    def _():
        finish(expert(gather_next))

    @pl.when(j + 1 == nused)
    def _():
        finish(expert(lambda k: None))


def _moe(h2, blk_e, nv, nused, tok, dst, w1, w3, w2, nblk, n_out_rows):
    d = D_MODEL
    idx_spec = lambda f: pl.BlockSpec((1, 1, MOE_BLOCK), f, memory_space=pltpu.SMEM)
    wspec = lambda r_, c_: pl.BlockSpec((None, r_, c_), lambda j, be, nv_, nu: (be[j], 0, 0))
    grid_spec = pltpu.PrefetchScalarGridSpec(
        num_scalar_prefetch=3,
        grid=(nblk,),
        in_specs=[idx_spec(lambda j, be, nv_, nu: (j, 0, 0)),
                  idx_spec(lambda j, be, nv_, nu: (jnp.minimum(j + 1, nblk - 1), 0, 0)),
                  idx_spec(lambda j, be, nv_, nu: (j, 0, 0)),
                  pl.BlockSpec(memory_space=pl.ANY),
                  wspec(d, EXPERT_FF), wspec(d, EXPERT_FF), wspec(EXPERT_FF, d)],
        out_specs=pl.BlockSpec(memory_space=pl.ANY),
        scratch_shapes=[pltpu.VMEM((2, MOE_BLOCK, d), f32), pltpu.VMEM((MOE_BLOCK, d), f32),
                        pltpu.SemaphoreType.DMA((2,)), pltpu.SemaphoreType.DMA(())],
    )
    return pl.pallas_call(
        _moe_kernel,
        out_shape=jax.ShapeDtypeStruct((n_out_rows, d), f32),
        grid_spec=grid_spec,
        compiler_params=_cparams("arbitrary"),
        name="moe_experts",
    )(blk_e, nv, nused, tok, tok, dst, h2, w1, w3, w2)


def _dispatch(route, n):
    nk = 2 * n
    flat_e = route[:, 0:2].astype(i32).reshape(-1)
    order = jnp.argsort(flat_e, stable=True).astype(i32)
    counts = jnp.sum(flat_e[:, None] == jnp.arange(N_EXPERTS, dtype=i32)[None, :], axis=0, dtype=i32)
    padded = (counts + MOE_BLOCK - 1) // MOE_BLOCK * MOE_BLOCK
    pend = jnp.cumsum(padded)
    pstart = pend - padded
    cstart = jnp.cumsum(counts) - counts
    nblk = nk // MOE_BLOCK + N_EXPERTS
    blk_start = jnp.arange(nblk, dtype=i32) * MOE_BLOCK
    blk_e = jnp.minimum(jnp.sum(pend[None, :] <= blk_start[:, None], axis=1, dtype=i32), N_EXPERTS - 1)
    off = blk_start - pstart[blk_e]
    nv = jnp.clip(counts[blk_e] - off, 0, MOE_BLOCK).astype(i32)
    within = jnp.arange(MOE_BLOCK, dtype=i32)[None, :]
    src = jnp.clip((cstart[blk_e] + off)[:, None] + within, 0, nk - 1)
    assign = order[src]
    valid = within < nv[:, None]
    tok = jnp.where(valid, assign // 2, 0)
    dst = jnp.where(valid, (assign % 2) * n + assign // 2, 0)
    nused = (pend[-1] // MOE_BLOCK).astype(i32).reshape(1)
    return (blk_e, nv, nused, tok.reshape(nblk, 1, MOE_BLOCK), dst.reshape(nblk, 1, MOE_BLOCK), nblk)


def _combine_kernel(emit_x, x1_ref, ya_ref, yb_ref, route_ref, g_ref, *outs):
    route = route_ref[...]
    x2 = x1_ref[...] + route[:, 2:3] * ya_ref[...] + route[:, 3:4] * yb_ref[...]
    hn = x2 * lax.rsqrt(jnp.mean(x2 * x2, axis=-1, keepdims=True) + RMS_EPS) * g_ref[...]
    if emit_x:
        outs[0][...] = x2
        outs[1][...] = hn.astype(outs[1].dtype)
    else:
        outs[0][...] = hn.astype(outs[0].dtype)


def _combine(x1, out2, route, g, emit_x, tm=512):
    n, d = x1.shape
    nt = n // tm
    tok = lambda c: pl.BlockSpec((tm, c), lambda i: (i, 0))
    if emit_x:
        out_shape = [jax.ShapeDtypeStruct((n, d), f32), jax.ShapeDtypeStruct((n, d), bf16)]
        out_specs = [tok(d), tok(d)]
    else:
        out_shape = [jax.ShapeDtypeStruct((n, d), f32)]
        out_specs = [tok(d)]
    return pl.pallas_call(
        functools.partial(_combine_kernel, emit_x),
        out_shape=out_shape,
        grid=(nt,),
        in_specs=[tok(d), tok(d), pl.BlockSpec((tm, d), lambda i: (i + nt, 0)), tok(LANES),
                  pl.BlockSpec((1, d), lambda i: (0, 0))],
        out_specs=out_specs,
        compiler_params=_cparams("parallel"),
        name="moe_combine",
    )(x1, out2, out2, route, g.reshape(1, d))


def _rope_table(T):
    pos = jnp.arange(T, dtype=f32)
    inv_freq = ROPE_THETA ** (-jnp.arange(0, HEAD, 2, dtype=f32) / HEAD)
    ang = pos[:, None] * inv_freq[None, :]
    cos, sin = jnp.cos(ang), jnp.sin(ang)
    cos_h = jnp.concatenate([cos, cos], axis=-1)
    sin_h = jnp.concatenate([-sin, sin], axis=-1)
    return jnp.concatenate([jnp.tile(cos_h, (1, 4)), jnp.tile(sin_h, (1, 4))], axis=-1)


def _pad_rows(w, rows, at=0):
    out = jnp.zeros((rows, w.shape[1]), w.dtype)
    return out.at[at:at + w.shape[0]].set(w)


def kernel(x, norm_mix, w_in, mu_a, w_decay0, w_decay2, a0, a2, g2, k_k, k_a, r_k, lnx_w, lnx_b, vres_mu, vres_w1, vres_w2, vres_b, sgu_ln_w, sgu_ln_b, sgu_ws, sgu_bs, gate_b, w_pa, w_pb, w_pc, w_out, norm_ffn, router_grp, router_grp_b, router_exp, router_exp_b, moe_w1, moe_w3, moe_w2, norm_final):
    bsz, T, d = x.shape
    n = bsz * T
    depth = w_in.shape[0]
    rope_tab = _rope_table(T)
    head_id = jnp.arange(LANES) // HEAD
    bd = (head_id[:, None] == head_id[None, :]).astype(bf16)
    row = lambda v: v.reshape(1, -1).astype(f32)

    xf = x.reshape(n, d)
    h = _rmsnorm(xf, norm_mix[0], bf16)
    v_first = None
    for l in range(depth):
        wl = w_in[l].astype(bf16)
        c0, c1, c2 = A_COLS, A_COLS + B_COLS, A_COLS + B_COLS + C_COLS
        zb = _proj(h, wl[:, c0:c1], B_COLS // 2, "proj_b")

        pa = dict(mu=row(mu_a[l]), w0=row(w_decay0[l]), w2=_pad_rows(w_decay2[l], LANES, 0).astype(bf16),
                  a0=row(a0[l]), a2=_pad_rows(a2[l], LANES, 64).astype(bf16), g2=g2[l].astype(bf16),
                  k_k=row(k_k[l]), k_a=row(k_a[l]), r_k=row(r_k[l]), bd=bd)
        vres = None
        if l > 0:
            w1p = jnp.zeros((d, LANES), f32).at[:, :vres_w1.shape[2]].set(vres_w1[l - 1]).astype(bf16)
            mup = jnp.zeros((1, LANES), f32).at[0, :vres_mu.shape[1]].set(vres_mu[l - 1])
            vres = dict(vfirst=v_first, w1=w1p, mu=mup,
                        w2=_pad_rows(vres_w2[l - 1], LANES, 0).astype(bf16), b=row(vres_b[l - 1]))
        rt, at, v, bT, kT, pcT, g, bonus = _wkv_prep(h, wl[:, :c0], bsz, T, pa, vres)
        if l == 0:
            v_first = v
        y = _wkv_scan(rt, at, v, bT, kT, pcT, bsz, T).reshape(n, d)

        yb = _attention(zb, rope_tab, bsz, T)

        bs_b = jnp.broadcast_to(sgu_bs[l][:, :, None], (SGU_GROUPS, SGU_CHUNK, LANES)).astype(f32)
        yc = _sgu(h, wl[:, c1:c2], row(sgu_ln_w[l]), row(sgu_ln_b[l]), sgu_ws[l], bs_b)

        rw = jnp.zeros((d, LANES), f32).at[:, :N_GROUPS].set(router_grp[l]) \
            .at[:, N_GROUPS:N_GROUPS + N_EXPERTS].set(router_exp[l])
        rw_hi = rw.astype(bf16)
        rw_lo = (rw - rw_hi.astype(f32)).astype(bf16)
        rb = jnp.zeros((1, LANES), f32).at[0, :N_GROUPS].set(router_grp_b[l]) \
            .at[0, N_GROUPS:N_GROUPS + N_EXPERTS].set(router_exp_b[l])
        pm = dict(lnx_w=row(lnx_w[l]), lnx_b=row(lnx_b[l]), gate_b=row(gate_b[l]), w_pa=w_pa[l].astype(bf16),
                  w_pb=w_pb[l].astype(bf16), w_pc=w_pc[l].astype(bf16), w_out=w_out[l].astype(bf16),
                  norm_ffn=row(norm_ffn[l]), bd=bd, rw_hi=rw_hi, rw_lo=rw_lo, rb=rb)
        x1, h2, route = _merge(xf, y, g.reshape(n, d), bonus.reshape(n, d), yb, yc, h, wl[:, c2:], pm)

        blk_e, nv, nused, tok, dst, nblk = _dispatch(route, n)
        out2 = _moe(h2, blk_e, nv, nused, tok, dst, moe_w1[l].astype(bf16), moe_w3[l].astype(bf16),
                    moe_w2[l].astype(bf16), nblk, 2 * n)
        if l + 1 < depth:
            xf, h = _combine(x1, out2, route, norm_mix[l + 1], True)
        else:
            (out,) = _combine(x1, out2, route, norm_final, False)
    return out.reshape(bsz, T, d)
```

```python
import functools
import math

import jax
import jax.numpy as jnp
from jax import lax
from jax.experimental import pallas as pl
from jax.experimental.pallas import tpu as pltpu

f32 = jnp.float32
bf16 = jnp.bfloat16
i32 = jnp.int32

D_MODEL = 1024
HEAD = 64
RMS_EPS = 1e-6
GN_EPS = 64e-5
LN_EPS = 1e-5
ROPE_THETA = 10000.0
A_WIDTH = D_MODEL
A_COLS = 3 * A_WIDTH + 64 + 64 + 128
B_COLS = 3 * 768
C_COLS = 2 * D_MODEL
G_COLS = 3 * D_MODEL
DILATIONS = (1, 4, 16)
ATT_BLOCK = 128
ATT_SPAN = 128
ATT_TILE = ATT_BLOCK * max(DILATIONS)
ATT_UNROLL = 2
SGU_CHUNK = 128
SGU_GROUPS = 8
N_GROUPS = 4
EXPERTS_PER_GROUP = 8
N_EXPERTS = N_GROUPS * EXPERTS_PER_GROUP
EXPERT_FF = D_MODEL // 2
MOE_BLOCK = 256
MOE_PIECE = 256
WKV_CHUNK = 128
SCAN_GROUP = 8
LANES = 128
VMEM_LIMIT = 48 * 1024 * 1024


def _cparams(*sem):
    return pltpu.CompilerParams(dimension_semantics=sem, vmem_limit_bytes=VMEM_LIMIT)


def _bdot(a, b):
    return jnp.dot(a.astype(bf16), b.astype(bf16), preferred_element_type=f32)


def _dot_hilo(a, b):
    hi = a.astype(bf16)
    lo = (a - hi.astype(f32)).astype(bf16)
    return (jnp.dot(hi, b, preferred_element_type=f32)
            + jnp.dot(lo, b, preferred_element_type=f32))


def _head_sum(x, bd):
    parts = [_dot_hilo(x[:, p * LANES:(p + 1) * LANES], bd) for p in range(x.shape[1] // LANES)]
    return jnp.concatenate(parts, axis=1)


def _sigmoid(x):
    return 0.5 + 0.5 * jnp.tanh(0.5 * x)


def _rmsnorm_kernel(x_ref, g_ref, o_ref):
    x = x_ref[...]
    y = x * lax.rsqrt(jnp.mean(x * x, axis=-1, keepdims=True) + RMS_EPS)
    o_ref[...] = (y * g_ref[...]).astype(o_ref.dtype)


def _rmsnorm(x, g, out_dtype, tm=1024):
    n, d = x.shape
    return pl.pallas_call(
        _rmsnorm_kernel,
        out_shape=jax.ShapeDtypeStruct((n, d), out_dtype),
        grid=(n // tm,),
        in_specs=[pl.BlockSpec((tm, d), lambda i: (i, 0)), pl.BlockSpec((1, d), lambda i: (0, 0))],
        out_specs=pl.BlockSpec((tm, d), lambda i: (i, 0)),
        compiler_params=_cparams("parallel"),
        name="rmsnorm",
    )(x, g.reshape(1, d))


def _matmul_kernel(h_ref, w_ref, o_ref):
    o_ref[...] = jnp.dot(h_ref[...], w_ref[...], preferred_element_type=f32).astype(o_ref.dtype)


def _proj(h, w, tn, name, tm=1024):
    n, k = h.shape
    m = w.shape[1]
    return pl.pallas_call(
        _matmul_kernel,
        out_shape=jax.ShapeDtypeStruct((n, m), bf16),
        grid=(m // tn, n // tm),
        in_specs=[pl.BlockSpec((tm, k), lambda j, i: (i, 0)), pl.BlockSpec((k, tn), lambda j, i: (0, j))],
        out_specs=pl.BlockSpec((tm, tn), lambda j, i: (i, j)),
        compiler_params=_cparams("parallel", "parallel"),
        name=name,
    )(h, w)


def _wkv_prep_kernel(has_vres, tm, *refs):
    if has_vres:
        (h_ref, wa_ref, vfirst_ref, mu_ref, w0_ref, w2_ref, a0_ref, a2_ref, g2_ref, kk_ref, ka_ref,
         rk_ref, bd_ref, shift_ref, tri_ref, vw1_ref, vmu_ref, vw2_ref, vb_ref,
         rt_ref, at_ref, v_ref, bT_ref, kT_ref, pcT_ref, g_ref, bonus_ref, carry, carry_v) = refs
    else:
        (h_ref, wa_ref, mu_ref, w0_ref, w2_ref, a0_ref, a2_ref, g2_ref, kk_ref, ka_ref, rk_ref, bd_ref,
         shift_ref, tri_ref,
         rt_ref, at_ref, v_ref, bT_ref, kT_ref, pcT_ref, g_ref, bonus_ref, carry) = refs
    t = pl.program_id(1)

    @pl.when(t == 0)
    def _():
        carry[...] = jnp.zeros_like(carry)
        if has_vres:
            carry_v[...] = jnp.zeros_like(carry_v)

    row = lax.broadcasted_iota(i32, (tm, 1), 0)
    zb = jnp.dot(h_ref[...], wa_ref[...], preferred_element_type=f32).astype(bf16)
    z = zb.astype(f32)
    shifted = jnp.dot(shift_ref[...], zb, preferred_element_type=f32)
    prev = jnp.where(row == 0, carry[0:1, :], shifted)
    carry[0:1, :] = z[tm - 1:tm, :]
    zs = z + (prev - z) * mu_ref[...]
    w = A_WIDTH
    r, k, v = zs[:, 0:w], zs[:, w:2 * w], zs[:, 2 * w:3 * w]
    lo = zs[:, 3 * w:3 * w + LANES]
    glo = zs[:, 3 * w + LANES:3 * w + 2 * LANES]

    if has_vres:
        hv = jnp.dot(h_ref[...], vw1_ref[...], preferred_element_type=f32)
        prev_v = jnp.where(row == 0, carry_v[0:1, :], pltpu.roll(hv, 1, axis=0))
        carry_v[0:1, :] = hv[tm - 1:tm, :]
        vlo = hv + (prev_v - hv) * vmu_ref[...]
        sg = _sigmoid(vb_ref[...] + _bdot(vlo, vw2_ref[...]))
        v = v + (vfirst_ref[...].astype(f32) - v) * sg

    wl = w0_ref[...] + _bdot(jnp.tanh(lo), w2_ref[...])
    yy = -wl
    softplus = jnp.maximum(yy, 0.0) + jnp.log(1.0 + jnp.exp(-jnp.abs(yy)))
    ld = -jnp.exp(-softplus - 0.5)
    a = _sigmoid(a0_ref[...] + _bdot(lo, a2_ref[...]))
    g = _bdot(_sigmoid(glo), g2_ref[...])
    bd = bd_ref[...]
    kk = k * kk_ref[...]
    kk = kk * lax.rsqrt(jnp.maximum(_head_sum(kk * kk, bd), 1e-24))
    kp = k * (1.0 + (a - 1.0) * ka_ref[...])
    bonus = _head_sum(r * kp * rk_ref[...], bd) * v

    tri = tri_ref[...]
    ld_hi = ld.astype(bf16)
    res = ld - ld_hi.astype(f32)
    ld_mid = res.astype(bf16)
    ld_lo = (res - ld_mid.astype(f32)).astype(bf16)
    cum = (jnp.dot(tri, ld_hi, preferred_element_type=f32) + jnp.dot(tri, ld_mid, preferred_element_type=f32)
           + jnp.dot(tri, ld_lo, preferred_element_type=f32))
    p_inc = jnp.exp(cum)
    p_inv = jnp.exp(-cum)
    ends = [cum[(c + 1) * WKV_CHUNK - 1:(c + 1) * WKV_CHUNK, :] for c in range(tm // WKV_CHUNK)]
    cum_end = ends[0]
    for c in range(1, tm // WKV_CHUNK):
        cum_end = jnp.where(row >= c * WKV_CHUNK, ends[c], cum_end)
    cum_end = jnp.broadcast_to(cum_end, (tm, w))

    rt_ref[...] = (r * p_inc).astype(bf16)
    at_ref[...] = (-kk * jnp.exp(cum - ld)).astype(bf16)
    v_ref[...] = v.astype(bf16)
    bT_ref[...] = (kk * a * p_inv).T.astype(bf16)
    kT_ref[...] = (kp * p_inv).T.astype(bf16)
    pcT_ref[...] = jnp.exp(cum_end).T
    g_ref[...] = g.astype(bf16)
    bonus_ref[...] = bonus.astype(bf16)


def _wkv_prep(h, wa, bsz, T, p, vres, tm=256):
    has_vres = vres is not None
    w = A_WIDTH
    tok = lambda cols: pl.BlockSpec((None, tm, cols), lambda b, t: (b, t, 0))
    par = lambda r_, c_: pl.BlockSpec((r_, c_), lambda b, t: (0, 0))
    ins = [h.reshape(bsz, T, D_MODEL), wa]
    specs = [tok(D_MODEL), pl.BlockSpec((D_MODEL, A_COLS), lambda b, t: (0, 0), pipeline_mode=pl.Buffered(1))]
    if has_vres:
        ins += [vres["vfirst"]]
        specs += [tok(w)]
    ti = jnp.arange(tm)
    shift = (ti[:, None] == ti[None, :] + 1).astype(bf16)
    tri = ((ti[:, None] >= ti[None, :]) & (ti[:, None] // WKV_CHUNK == ti[None, :] // WKV_CHUNK)).astype(bf16)
    ins += [p["mu"], p["w0"], p["w2"], p["a0"], p["a2"], p["g2"], p["k_k"], p["k_a"], p["r_k"], p["bd"],
            shift, tri]
    specs += [par(1, A_COLS), par(1, w), par(LANES, w), par(1, w), par(LANES, w), par(LANES, w),
              par(1, w), par(1, w), par(1, w), par(LANES, LANES), par(tm, tm), par(tm, tm)]
    if has_vres:
        ins += [vres["w1"], vres["mu"], vres["w2"], vres["b"]]
        specs += [par(D_MODEL, LANES), par(1, LANES), par(LANES, w), par(1, w)]
    tr = pl.BlockSpec((None, w, tm), lambda b, t: (b, 0, t))
    out_shape = [jax.ShapeDtypeStruct((bsz, T, w), bf16)] * 3 + \
                [jax.ShapeDtypeStruct((bsz, w, T), bf16)] * 2 + \
                [jax.ShapeDtypeStruct((bsz, w, T), f32)] + \
                [jax.ShapeDtypeStruct((bsz, T, w), bf16)] * 2
    out_specs = [tok(w)] * 3 + [tr] * 3 + [tok(w)] * 2
    scratch = [pltpu.VMEM((8, A_COLS), f32)]
    if has_vres:
        scratch.append(pltpu.VMEM((8, LANES), f32))
    return pl.pallas_call(
        functools.partial(_wkv_prep_kernel, has_vres, tm),
        out_shape=out_shape,
        grid=(bsz, T // tm),
        in_specs=specs,
        out_specs=out_specs,
        scratch_shapes=scratch,
        compiler_params=_cparams("parallel", "arbitrary"),
        name="wkv_prep",
    )(*ins)


def _wkv_scan_kernel(rt_ref, at_ref, v_ref, bT_ref, kT_ref, pcT_ref, y_ref, st_ref):
    C = WKV_CHUNK

    @pl.when(pl.program_id(1) == 0)
    def _():
        st_ref[...] = jnp.zeros_like(st_ref)

    head0 = lax.broadcasted_iota(i32, (1, LANES), 1) < HEAD
    key0 = lax.broadcasted_iota(i32, (LANES, 1), 0) < HEAD
    same_head = head0 == key0
    ri = lax.broadcasted_iota(i32, (C, C), 0)
    ci = lax.broadcasted_iota(i32, (C, C), 1)
    strict = ri > ci
    lower2 = jnp.concatenate([ri >= ci, ri >= ci], axis=1)
    eye = (ri == ci).astype(f32)

    fdot = lambda a, b: jnp.dot(a, b, preferred_element_type=f32)
    hms = (head0, jnp.logical_not(head0))

    def pair_group(gi, carry_):
        pis = [gi * SCAN_GROUP + q for q in range(SCAN_GROUP)]
        c0s = [pl.multiple_of(pi * LANES, LANES) for pi in pis]
        ars = [jnp.concatenate([at_ref[:, pl.ds(c0, LANES)], rt_ref[:, pl.ds(c0, LANES)]], axis=0) for c0 in c0s]
        vs = [v_ref[:, pl.ds(c0, LANES)] for c0 in c0s]
        bks = [jnp.concatenate([bT_ref[pl.ds(c0, LANES), :], kT_ref[pl.ds(c0, LANES), :]], axis=1) for c0 in c0s]
        sts = [st_ref[pi] for pi in pis]
        xss = [fdot(ar, st.astype(bf16)) for ar, st in zip(ars, sts)]
        heads = [(q, hm) for q in range(SCAN_GROUP) for hm in hms]
        grams = [fdot(jnp.where(hm, ars[q], 0), bks[q]) for q, hm in heads]
        npows = [jnp.where(strict, gm[0:C, 0:C], 0.0) for gm in grams]
        a_aks = [jnp.where(strict, gm[0:C, C:2 * C], 0.0).astype(bf16) for gm in grams]
        a_rs = [jnp.where(lower2, gm[C:2 * C, :], 0.0).astype(bf16) for gm in grams]
        xins = [xss[q][0:C] + fdot(a_ak, vs[q]) for (q, _), a_ak in zip(heads, a_aks)]
        winvs = [eye + n for n in npows]
        s = 2
        while s < C:
            npows = [_bdot(n, n) for n in npows]
            winvs = [w + _bdot(w, n) for w, n in zip(winvs, npows)]
            s *= 2
        us = [_bdot(w, x) for w, x in zip(winvs, xins)]
        for q in range(SCAN_GROUP):
            uv = jnp.concatenate([jnp.where(head0, us[2 * q], us[2 * q + 1]).astype(bf16), vs[q]], axis=0)
            y_ref[:, pl.ds(c0s[q], LANES)] = xss[q][C:2 * C] + jnp.where(
                head0, fdot(a_rs[2 * q], uv), fdot(a_rs[2 * q + 1], uv))
            pc = pcT_ref[pl.ds(c0s[q], LANES), :]
            bkp = (bks[q].astype(f32) * jnp.concatenate([pc, pc], axis=1)).astype(bf16)
            st_ref[pis[q]] = sts[q] * pc + jnp.where(same_head, fdot(bkp, uv), 0.0)
        return carry_

    lax.fori_loop(0, A_WIDTH // LANES // SCAN_GROUP, pair_group, 0)


def _wkv_scan(rt, at, v, bT, kT, pcT, bsz, T):
    C = WKV_CHUNK
    w = A_WIDTH
    tok = pl.BlockSpec((None, C, w), lambda b, t: (b, t, 0))
    tr = pl.BlockSpec((None, w, C), lambda b, t: (b, 0, t))
    return pl.pallas_call(
        _wkv_scan_kernel,
        out_shape=jax.ShapeDtypeStruct((bsz, T, w), f32),
        grid=(bsz, T // C),
        in_specs=[tok, tok, tok, tr, tr, tr],
        out_specs=tok,
        scratch_shapes=[pltpu.VMEM((w // LANES, LANES, LANES), f32)],
        compiler_params=_cparams("parallel", "arbitrary"),
        name="wkv_scan",
    )(rt, at, v, bT, kT, pcT)


def _attn_kernel(q_ref, k_ref, v_ref, cos_ref, sin_ref, o_ref,
                 qs, kall, vall, ktails, vtails, m_acc, num_acc, den_acc):
    t = pl.program_id(1)
    g = pl.program_id(2)
    TT = ATT_TILE
    blk = ATT_BLOCK
    width = q_ref.shape[-1]
    lane = lax.broadcasted_iota(i32, (1, width), 1)
    first_half = (lane % HEAD) < (HEAD // 2)
    cos = cos_ref[...]
    sin = sin_ref[...]

    def rope(z):
        partner = jnp.where(first_half, pltpu.roll(z, width - HEAD // 2, axis=1),
                            pltpu.roll(z, HEAD // 2, axis=1))
        return z * cos + partner * sin

    halves = width // LANES

    def put(ref, r0, val):
        for c in range(halves):
            ref[c, r0:r0 + val.shape[0], :] = val[:, c * LANES:(c + 1) * LANES]

    def take(ref, r0, n):
        return jnp.concatenate([ref[c, r0:r0 + n, :] for c in range(halves)], axis=1)

    put(qs, 0, rope(q_ref[...].astype(f32)) * (HEAD ** -0.5))
    put(kall, TT, rope(k_ref[...].astype(f32)))
    put(vall, TT, v_ref[...].astype(f32))

    qi = lax.broadcasted_iota(i32, (blk, 2 * blk), 0)
    kj = lax.broadcasted_iota(i32, (blk, 2 * blk), 1)
    dist = qi + blk - kj
    band = (dist >= 0) & (dist <= ATT_SPAN)
    cur_keys = kj >= blk
    head_masks = [(lane // HEAD) == h for h in range(width // HEAD)]

    def group(gi, d):
        tail = blk * d
        ktail, vtail = ktails[gi], vtails[gi]

        @pl.when(t == 0)
        def _():
            put(kall, TT - tail, jnp.zeros((tail, width), f32))
            put(vall, TT - tail, jnp.zeros((tail, width), f32))

        @pl.when(t > 0)
        def _():
            put(kall, TT - tail, ktail[...])
            put(vall, TT - tail, vtail[...])

        def unit_group(ug, carry_):
            units = []
            for j in range(ATT_UNROLL):
                u = ug * ATT_UNROLL + j
                nl = u // d
                start = u % d + tail * nl
                rows = functools.partial(lambda s0, base: pl.ds(base + s0, blk, stride=d), start)
                ld = functools.partial(
                    lambda rw, ref, base: jnp.concatenate([ref[c, rw(base), :] for c in range(halves)], axis=1), rows)
                q = ld(qs, 0)
                kcat = jnp.concatenate([ld(kall, TT - tail), ld(kall, TT)], axis=0).astype(bf16)
                vcat = jnp.concatenate([ld(vall, TT - tail), ld(vall, TT)], axis=0).astype(bf16)
                mask = band & (cur_keys | jnp.logical_or(t > 0, nl > 0))
                units.append((rows, ld, q, kcat, vcat, mask))
            chains = [(un, hm) for un in units for hm in head_masks]
            ss = [lax.dot_general(jnp.where(hm, un[2], 0.0).astype(bf16), un[3], (((1,), (1,)), ((), ())),
                                  preferred_element_type=f32) for un, hm in chains]
            ss = [jnp.where(un[5], s, -jnp.inf) for (un, _), s in zip(chains, ss)]
            ms = [jnp.max(s, axis=-1, keepdims=True) for s in ss]
            es = [jnp.exp(s - m) for s, m in zip(ss, ms)]
            dens = [jnp.sum(e, axis=-1, keepdims=True) for e in es]
            pvs = [jnp.dot(e.astype(bf16), un[4], preferred_element_type=f32) for (un, _), e in zip(chains, es)]
            nh = len(head_masks)
            for j, (rows, ld, _, _, _, _) in enumerate(units):
                num, m_full, den_full = pvs[j * nh], ms[j * nh], dens[j * nh]
                for h in range(1, nh):
                    hm = head_masks[h]
                    num = jnp.where(hm, pvs[j * nh + h], num)
                    m_full = jnp.where(hm, ms[j * nh + h], m_full)
                    den_full = jnp.where(hm, dens[j * nh + h], den_full)

                def st(ref, val, rows=rows):
                    for c in range(halves):
                        ref[c, rows(0), :] = val[:, c * LANES:(c + 1) * LANES]

                if gi == 0:
                    st(m_acc, m_full)
                    st(num_acc, num)
                    st(den_acc, den_full)
                else:
                    m_old = ld(m_acc, 0)
                    m_new = jnp.maximum(m_old, m_full)
                    w_old = jnp.exp(m_old - m_new)
                    w_new = jnp.exp(m_full - m_new)
                    st(m_acc, m_new)
                    st(num_acc, ld(num_acc, 0) * w_old + num * w_new)
                    st(den_acc, ld(den_acc, 0) * w_old + den_full * w_new)
            return carry_

        lax.fori_loop(0, TT // blk // ATT_UNROLL, unit_group, 0)
        ktail[...] = take(kall, 2 * TT - tail, tail)
        vtail[...] = take(vall, 2 * TT - tail, tail)

    for gi, d in enumerate(DILATIONS):
        pl.when(g == gi)(functools.partial(group, gi, d))

    @pl.when(g == len(DILATIONS) - 1)
    def _():
        o_ref[...] = (take(num_acc, 0, TT) / take(den_acc, 0, TT)).astype(o_ref.dtype)


def _attention(zb, rope_tab, bsz, T):
    TT = ATT_TILE
    width = 4 * HEAD
    zv = zb.reshape(bsz, T, B_COLS)
    col = lambda off: pl.BlockSpec((None, TT, width), lambda b, t, g: (b, t, off + g))
    tab = lambda off: pl.BlockSpec((TT, width), lambda b, t, g: (t, off))
    tails = [pltpu.VMEM((ATT_BLOCK * d, width), f32) for d in DILATIONS]
    tiled = lambda rows: pltpu.VMEM((width // LANES, rows, LANES), f32)
    out = pl.pallas_call(
        _attn_kernel,
        out_shape=jax.ShapeDtypeStruct((bsz, T, width), bf16),
        grid=(bsz, T // TT, len(DILATIONS)),
        in_specs=[col(0), col(3), col(6), tab(0), tab(1)],
        out_specs=pl.BlockSpec((None, TT, width), lambda b, t, g: (b, t, 0)),
        scratch_shapes=[tiled(TT), tiled(2 * TT), tiled(2 * TT), tails, list(tails),
                        tiled(TT), tiled(TT), tiled(TT)],
        compiler_params=_cparams("parallel", "arbitrary", "arbitrary"),
        name="dilated_attn",
    )(zv, zv, zv, rope_tab, rope_tab)
    return out.reshape(bsz * T, width)


def _sgu_kernel(tm, h_ref, wc_ref, lnw_ref, lnb_ref, ws_ref, bs_ref, o_ref):
    ch = SGU_CHUNK
    z = jnp.dot(h_ref[...], wc_ref[...], preferred_element_type=f32).astype(bf16).astype(f32)
    ge = 0.5 * z * (1.0 + lax.erf(z * (2.0 ** -0.5)))
    u = ge[:, :D_MODEL]
    v = ge[:, D_MODEL:]
    mean = jnp.mean(v, axis=-1, keepdims=True)
    var = jnp.mean(jnp.square(v - mean), axis=-1, keepdims=True)
    vn = ((v - mean) * lax.rsqrt(var + LN_EPS) * lnw_ref[...] + lnb_ref[...]).astype(bf16)
    ri = lax.broadcasted_iota(i32, (ch, ch), 0)
    ci = lax.broadcasted_iota(i32, (ch, ch), 1)
    causal = ri >= ci
    for g in range(SGU_GROUPS):
        wg = jnp.where(causal, ws_ref[g], 0.0).astype(bf16)
        bias = bs_ref[g]
        for c in range(tm // ch):
            rows = slice(c * ch, (c + 1) * ch)
            cols = slice(g * LANES, (g + 1) * LANES)
            mixed = jnp.dot(wg, vn[rows, cols], preferred_element_type=f32) + bias
            o_ref[rows, cols] = (u[rows, cols] * mixed).astype(o_ref.dtype)


def _sgu(h, wc, lnw, lnb, ws, bs_b, tm=512):
    n = h.shape[0]
    return pl.pallas_call(
        functools.partial(_sgu_kernel, tm),
        out_shape=jax.ShapeDtypeStruct((n, D_MODEL), bf16),
        grid=(n // tm,),
        in_specs=[pl.BlockSpec((tm, D_MODEL), lambda i: (i, 0)),
                  pl.BlockSpec((D_MODEL, C_COLS), lambda i: (0, 0), pipeline_mode=pl.Buffered(1)),
                  pl.BlockSpec((1, D_MODEL), lambda i: (0, 0)),
                  pl.BlockSpec((1, D_MODEL), lambda i: (0, 0)),
                  pl.BlockSpec((SGU_GROUPS, SGU_CHUNK, SGU_CHUNK), lambda i: (0, 0, 0)),
                  pl.BlockSpec((SGU_GROUPS, SGU_CHUNK, LANES), lambda i: (0, 0, 0))],
        out_specs=pl.BlockSpec((tm, D_MODEL), lambda i: (i, 0)),
        compiler_params=_cparams("parallel"),
        name="sgu",
    )(h, wc, lnw, lnb, ws, bs_b)


def _merge_kernel(x_ref, y_ref, g_ref, bonus_ref, yb_ref, yc_ref, h_ref, wg_ref, lnxw_ref, lnxb_ref, gb_ref,
                  wpa_ref, wpb_ref, wpc_ref, wout_ref, nf_ref, bd_ref, rwh_ref, rwl_ref, rb_ref, x1_ref, h2_ref, route_ref):
    bd = bd_ref[...]
    y = y_ref[...]
    mean = _head_sum(y, bd) * (1.0 / HEAD)
    dy = y - mean
    var = _head_sum(dy * dy, bd) * (1.0 / HEAD)
    yn = dy * lax.rsqrt(var + GN_EPS) * lnxw_ref[...] + lnxb_ref[...] + bonus_ref[...].astype(f32)
    ya = (yn * g_ref[...].astype(f32)).astype(bf16)

    zg = jnp.dot(h_ref[...], wg_ref[...], preferred_element_type=f32).astype(bf16)
    gates = _sigmoid(zg.astype(f32) + gb_ref[...])
    d = D_MODEL
    merged = (gates[:, 0:d] * jnp.dot(ya, wpa_ref[...], preferred_element_type=f32)
              + gates[:, d:2 * d] * jnp.dot(yb_ref[...], wpb_ref[...], preferred_element_type=f32)
              + gates[:, 2 * d:3 * d] * jnp.dot(yc_ref[...], wpc_ref[...], preferred_element_type=f32))
    x1 = x_ref[...] + _bdot(merged, wout_ref[...])
    x1_ref[...] = x1
    hn = x1 * lax.rsqrt(jnp.mean(x1 * x1, axis=-1, keepdims=True) + RMS_EPS) * nf_ref[...]
    h2_ref[...] = hn

    hh = hn.astype(bf16)
    hl = (hn - hh.astype(f32)).astype(bf16)
    logits = (jnp.dot(hh, rwh_ref[...], preferred_element_type=f32)
              + jnp.dot(hl, rwh_ref[...], preferred_element_type=f32)
              + jnp.dot(hh, rwl_ref[...], preferred_element_type=f32)) + rb_ref[...]
    lane_i = lax.broadcasted_iota(i32, logits.shape, 1)
    lane = lane_i.astype(f32)
    lane_grp = ((lane_i - N_GROUPS) // EXPERTS_PER_GROUP).astype(f32)
    neg = jnp.float32(-1e30)
    big = jnp.float32(1e9)
    is_g = lane_i < N_GROUPS
    gl = jnp.where(is_g, logits, neg)
    gmax = jnp.max(gl, axis=-1, keepdims=True)
    gsel = jnp.min(jnp.where(gl == gmax, lane, big), axis=-1, keepdims=True)
    gsum = jnp.sum(jnp.where(is_g, jnp.exp(gl - gmax), 0.0), axis=-1, keepdims=True)
    pg = 1.0 / gsum
    emask = (lane_i >= N_GROUPS) & (lane_i < N_GROUPS + N_EXPERTS) & (lane_grp == gsel)
    el = jnp.where(emask, logits, neg)
    t1 = jnp.max(el, axis=-1, keepdims=True)
    i1 = jnp.min(jnp.where(el == t1, lane, big), axis=-1, keepdims=True)
    el2 = jnp.where(lane == i1, neg, el)
    t2 = jnp.max(el2, axis=-1, keepdims=True)
    i2 = jnp.min(jnp.where(el2 == t2, lane, big), axis=-1, keepdims=True)
    e21 = jnp.exp(t2 - t1)
    wa = pg / (1.0 + e21)
    wb = pg * e21 / (1.0 + e21)
    route = jnp.where(lane_i == 0, i1 - N_GROUPS,
                      jnp.where(lane_i == 1, i2 - N_GROUPS,
                                jnp.where(lane_i == 2, wa, jnp.where(lane_i == 3, wb, 0.0))))
    route_ref[...] = route


def _merge(x, y, g, bonus, yb, yc, h, wg, p, tm=512):
    n = x.shape[0]
    d = D_MODEL
    tok = lambda c: pl.BlockSpec((tm, c), lambda i: (i, 0))
    par = lambda r_, c_: pl.BlockSpec((r_, c_), lambda i: (0, 0))
    return pl.pallas_call(
        _merge_kernel,
        out_shape=[jax.ShapeDtypeStruct((n, d), f32), jax.ShapeDtypeStruct((n, d), f32),
                   jax.ShapeDtypeStruct((n, LANES), f32)],
        grid=(n // tm,),
        in_specs=[tok(d), tok(d), tok(d), tok(d), tok(256), tok(d), tok(d),
                  pl.BlockSpec((d, G_COLS), lambda i: (0, 0), pipeline_mode=pl.Buffered(1)),
                  par(1, d), par(1, d), par(1, G_COLS), par(d, d), par(256, d),
                  par(d, d), par(d, d), par(1, d), par(LANES, LANES), par(d, LANES), par(d, LANES),
                  par(1, LANES)],
        out_specs=[tok(d), tok(d), tok(LANES)],
        compiler_params=_cparams("parallel"),
        name="merge_router",
    )(x, y, g, bonus, yb, yc, h, wg, p["lnx_w"], p["lnx_b"], p["gate_b"], p["w_pa"], p["w_pb"],
      p["w_pc"], p["w_out"], p["norm_ffn"], p["bd"], p["rw_hi"], p["rw_lo"], p["rb"])


def _moe_kernel(blk_e_ref, nv_ref, nused_ref, tok_ref, tok_next_ref, dst_ref, dst_prev_ref, h2_hbm,
                w1_ref, w3_ref, w2_ref, out_hbm, xbuf, ybuf, sem_in, sem_out):
    j = pl.program_id(0)
    rows = MOE_BLOCK
    nused = nused_ref[0]
    slot = j % 2
    other = 1 - slot
    prev_nv = nv_ref[jnp.maximum(j - 1, 0)]
    prev_full = jnp.logical_and(j >= 1, prev_nv == rows)

    def start_gather(idx_ref, buf):
        def body(i, c):
            pltpu.make_async_copy(h2_hbm.at[pl.ds(idx_ref[0, 0, i], 1), :],
                                  xbuf.at[buf, pl.ds(i, 1), :], sem_in.at[buf]).start()
            return c
        lax.fori_loop(0, rows, body, 0, unroll=8)

    def scatter_row(idx_ref, buf, i):
        pltpu.make_async_copy(ybuf.at[buf, pl.ds(i, 1), :],
                              out_hbm.at[pl.ds(idx_ref[0, 0, i], 1), :], sem_out.at[buf]).start()

    def scatter_loop(idx_ref, buf, count):
        def body(i, c):
            scatter_row(idx_ref, buf, i)
            return c
        lax.fori_loop(0, count, body, 0)

    def wait_scatter(buf, count):
        bulk = pl.multiple_of((count // 8) * 8, 8)

        @pl.when(bulk > 0)
        def _():
            pltpu.make_async_copy(ybuf.at[buf, pl.ds(0, bulk), :], out_hbm.at[pl.ds(0, bulk), :],
                                  sem_out.at[buf]).wait()

        def body(i, c):
            pltpu.make_async_copy(ybuf.at[buf, pl.ds(0, 1), :], out_hbm.at[pl.ds(0, 1), :],
                                  sem_out.at[buf]).wait()
            return c
        lax.fori_loop(0, count - bulk, body, 0)

    @pl.when(j == 0)
    def _():
        start_gather(tok_ref, 0)

    @pl.when(jnp.logical_and(jnp.logical_and(j >= 1, j < nused), prev_nv < rows))
    def _():
        scatter_loop(dst_prev_ref, other, prev_nv)

    def expert(between):
        pltpu.make_async_copy(h2_hbm.at[pl.ds(0, rows), :], xbuf.at[slot], sem_in.at[slot]).wait()
        x = xbuf[slot].astype(bf16)
        piece = 0
        hid = []
        for c in range(EXPERT_FF // MOE_PIECE):
            cols = slice(c * MOE_PIECE, (c + 1) * MOE_PIECE)
            h1 = jnp.dot(x, w1_ref[:, cols], preferred_element_type=f32)
            between(piece)
            h3 = jnp.dot(x, w3_ref[:, cols], preferred_element_type=f32)
            between(piece + 1)
            piece += 2
            hid.append((h1 * _sigmoid(h1) * h3).astype(bf16))
        hid = jnp.concatenate(hid, axis=1)
        ys = []
        for c in range(D_MODEL // MOE_PIECE):
            ys.append(jnp.dot(hid, w2_ref[:, c * MOE_PIECE:(c + 1) * MOE_PIECE], preferred_element_type=f32))
            between(piece)
            piece += 1
        return jnp.concatenate(ys, axis=1)

    n_pieces = 2 * (EXPERT_FF // MOE_PIECE) + D_MODEL // MOE_PIECE
    per_piece = rows // n_pieces

    def issue_between(gather_next, scatter_prev):
        def between(k):
            for i in range(k * per_piece, (k + 1) * per_piece):
                if gather_next:
                    pltpu.make_async_copy(h2_hbm.at[pl.ds(tok_next_ref[0, 0, i], 1), :],
                                          xbuf.at[other, pl.ds(i, 1), :], sem_in.at[other]).start()
                if scatter_prev:
                    scatter_row(dst_prev_ref, other, i)
        return between

    def finish(y):
        @pl.when(j >= 2)
        def _():
            wait_scatter(slot, nv_ref[jnp.maximum(j - 2, 0)])

        ybuf[slot] = y

        @pl.when(j == nused - 1)
        def _():
            nv = nv_ref[j]
            scatter_loop(dst_ref, slot, nv)

            @pl.when(j >= 1)
            def _():
                wait_scatter(other, prev_nv)

            wait_scatter(slot, nv)

    has_next = j + 1 < nused
    is_last = j + 1 == nused
    not_full = jnp.logical_not(prev_full)
    for cond, gather_next, scatter_prev in ((jnp.logical_and(has_next, prev_full), True, True),
                                            (jnp.logical_and(has_next, not_full), True, False),
                                            (jnp.logical_and(is_last, prev_full), False, True),
                                            (jnp.logical_and(is_last, not_full), False, False)):
        pl.when(cond)(lambda g_=gather_next, s_=scatter_prev: finish(expert(issue_between(g_, s_))))


def _moe(h2, blk_e, nv, nused, tok, dst, w1, w3, w2, nblk, n_out_rows):
    d = D_MODEL
    idx_spec = lambda f: pl.BlockSpec((1, 1, MOE_BLOCK), f, memory_space=pltpu.SMEM)
    wspec = lambda r_, c_: pl.BlockSpec((None, r_, c_), lambda j, be, nv_, nu: (be[j], 0, 0))
    grid_spec = pltpu.PrefetchScalarGridSpec(
        num_scalar_prefetch=3,
        grid=(nblk,),
        in_specs=[idx_spec(lambda j, be, nv_, nu: (j, 0, 0)),
                  idx_spec(lambda j, be, nv_, nu: (jnp.minimum(j + 1, nblk - 1), 0, 0)),
                  idx_spec(lambda j, be, nv_, nu: (j, 0, 0)),
                  idx_spec(lambda j, be, nv_, nu: (jnp.maximum(j - 1, 0), 0, 0)),
                  pl.BlockSpec(memory_space=pl.ANY),
                  wspec(d, EXPERT_FF), wspec(d, EXPERT_FF), wspec(EXPERT_FF, d)],
        out_specs=pl.BlockSpec(memory_space=pl.ANY),
        scratch_shapes=[pltpu.VMEM((2, MOE_BLOCK, d), f32), pltpu.VMEM((2, MOE_BLOCK, d), f32),
                        pltpu.SemaphoreType.DMA((2,)), pltpu.SemaphoreType.DMA((2,))],
    )
    return pl.pallas_call(
        _moe_kernel,
        out_shape=jax.ShapeDtypeStruct((n_out_rows, d), f32),
        grid_spec=grid_spec,
        compiler_params=_cparams("arbitrary"),
        name="moe_experts",
    )(blk_e, nv, nused, tok, tok, dst, dst, h2, w1, w3, w2)


def _dispatch(route, n):
    nk = 2 * n
    flat_e = route[:, 0:2].astype(i32).reshape(-1)
    order = jnp.argsort(flat_e, stable=True).astype(i32)
    counts = jnp.sum(flat_e[:, None] == jnp.arange(N_EXPERTS, dtype=i32)[None, :], axis=0, dtype=i32)
    padded = (counts + MOE_BLOCK - 1) // MOE_BLOCK * MOE_BLOCK
    pend = jnp.cumsum(padded)
    pstart = pend - padded
    cstart = jnp.cumsum(counts) - counts
    nblk = nk // MOE_BLOCK + N_EXPERTS
    blk_start = jnp.arange(nblk, dtype=i32) * MOE_BLOCK
    blk_e = jnp.minimum(jnp.sum(pend[None, :] <= blk_start[:, None], axis=1, dtype=i32), N_EXPERTS - 1)
    off = blk_start - pstart[blk_e]
    nv = jnp.clip(counts[blk_e] - off, 0, MOE_BLOCK).astype(i32)
    within = jnp.arange(MOE_BLOCK, dtype=i32)[None, :]
    src = jnp.clip((cstart[blk_e] + off)[:, None] + within, 0, nk - 1)
    assign = order[src]
    valid = within < nv[:, None]
    tok = jnp.where(valid, assign // 2, 0)
    dst = jnp.where(valid, (assign % 2) * n + assign // 2, 0)
    nused = (pend[-1] // MOE_BLOCK).astype(i32).reshape(1)
    return (blk_e, nv, nused, tok.reshape(nblk, 1, MOE_BLOCK), dst.reshape(nblk, 1, MOE_BLOCK), nblk)


def _combine_kernel(emit_x, x1_ref, ya_ref, yb_ref, route_ref, g_ref, *outs):
    route = route_ref[...]
    x2 = x1_ref[...] + route[:, 2:3] * ya_ref[...] + route[:, 3:4] * yb_ref[...]
    hn = x2 * lax.rsqrt(jnp.mean(x2 * x2, axis=-1, keepdims=True) + RMS_EPS) * g_ref[...]
    if emit_x:
        outs[0][...] = x2
        outs[1][...] = hn.astype(outs[1].dtype)
    else:
        outs[0][...] = hn.astype(outs[0].dtype)


def _combine(x1, out2, route, g, emit_x, tm=512):
    n, d = x1.shape
    nt = n // tm
    tok = lambda c: pl.BlockSpec((tm, c), lambda i: (i, 0))
    if emit_x:
        out_shape = [jax.ShapeDtypeStruct((n, d), f32), jax.ShapeDtypeStruct((n, d), bf16)]
        out_specs = [tok(d), tok(d)]
    else:
        out_shape = [jax.ShapeDtypeStruct((n, d), f32)]
        out_specs = [tok(d)]
    return pl.pallas_call(
        functools.partial(_combine_kernel, emit_x),
        out_shape=out_shape,
        grid=(nt,),
        in_specs=[tok(d), tok(d), pl.BlockSpec((tm, d), lambda i: (i + nt, 0)), tok(LANES),
                  pl.BlockSpec((1, d), lambda i: (0, 0))],
        out_specs=out_specs,
        compiler_params=_cparams("parallel"),
        name="moe_combine",
    )(x1, out2, out2, route, g.reshape(1, d))


def _rope_table(T):
    pos = jnp.arange(T, dtype=f32)
    inv_freq = ROPE_THETA ** (-jnp.arange(0, HEAD, 2, dtype=f32) / HEAD)
    ang = pos[:, None] * inv_freq[None, :]
    cos, sin = jnp.cos(ang), jnp.sin(ang)
    cos_h = jnp.concatenate([cos, cos], axis=-1)
    sin_h = jnp.concatenate([-sin, sin], axis=-1)
    return jnp.concatenate([jnp.tile(cos_h, (1, 4)), jnp.tile(sin_h, (1, 4))], axis=-1)


def _pad_rows(w, rows, at=0):
    out = jnp.zeros((rows, w.shape[1]), w.dtype)
    return out.at[at:at + w.shape[0]].set(w)


def kernel(x, norm_mix, w_in, mu_a, w_decay0, w_decay2, a0, a2, g2, k_k, k_a, r_k, lnx_w, lnx_b, vres_mu, vres_w1, vres_w2, vres_b, sgu_ln_w, sgu_ln_b, sgu_ws, sgu_bs, gate_b, w_pa, w_pb, w_pc, w_out, norm_ffn, router_grp, router_grp_b, router_exp, router_exp_b, moe_w1, moe_w3, moe_w2, norm_final):
    bsz, T, d = x.shape
    n = bsz * T
    depth = w_in.shape[0]
    rope_tab = _rope_table(T)
    head_id = jnp.arange(LANES) // HEAD
    bd = (head_id[:, None] == head_id[None, :]).astype(bf16)
    row = lambda v: v.reshape(1, -1).astype(f32)

    xf = x.reshape(n, d)
    h = _rmsnorm(xf, norm_mix[0], bf16)
    v_first = None
    for l in range(depth):
        wl = w_in[l].astype(bf16)
        c0, c1, c2 = A_COLS, A_COLS + B_COLS, A_COLS + B_COLS + C_COLS
        zb = _proj(h, wl[:, c0:c1], B_COLS // 2, "proj_b")

        pa = dict(mu=row(mu_a[l]), w0=row(w_decay0[l]), w2=_pad_rows(w_decay2[l], LANES, 0).astype(bf16),
                  a0=row(a0[l]), a2=_pad_rows(a2[l], LANES, 64).astype(bf16), g2=g2[l].astype(bf16),
                  k_k=row(k_k[l]), k_a=row(k_a[l]), r_k=row(r_k[l]), bd=bd)
        vres = None
        if l > 0:
            w1p = jnp.zeros((d, LANES), f32).at[:, :vres_w1.shape[2]].set(vres_w1[l - 1]).astype(bf16)
            mup = jnp.zeros((1, LANES), f32).at[0, :vres_mu.shape[1]].set(vres_mu[l - 1])
            vres = dict(vfirst=v_first, w1=w1p, mu=mup,
                        w2=_pad_rows(vres_w2[l - 1], LANES, 0).astype(bf16), b=row(vres_b[l - 1]))
        rt, at, v, bT, kT, pcT, g, bonus = _wkv_prep(h, wl[:, :c0], bsz, T, pa, vres)
        if l == 0:
            v_first = v
        y = _wkv_scan(rt, at, v, bT, kT, pcT, bsz, T).reshape(n, d)

        yb = _attention(zb, rope_tab, bsz, T)

        bs_b = jnp.broadcast_to(sgu_bs[l][:, :, None], (SGU_GROUPS, SGU_CHUNK, LANES)).astype(f32)
        yc = _sgu(h, wl[:, c1:c2], row(sgu_ln_w[l]), row(sgu_ln_b[l]), sgu_ws[l], bs_b)

        rw = jnp.zeros((d, LANES), f32).at[:, :N_GROUPS].set(router_grp[l]) \
            .at[:, N_GROUPS:N_GROUPS + N_EXPERTS].set(router_exp[l])
        rw_hi = rw.astype(bf16)
        rw_lo = (rw - rw_hi.astype(f32)).astype(bf16)
        rb = jnp.zeros((1, LANES), f32).at[0, :N_GROUPS].set(router_grp_b[l]) \
            .at[0, N_GROUPS:N_GROUPS + N_EXPERTS].set(router_exp_b[l])
        pm = dict(lnx_w=row(lnx_w[l]), lnx_b=row(lnx_b[l]), gate_b=row(gate_b[l]), w_pa=w_pa[l].astype(bf16),
                  w_pb=w_pb[l].astype(bf16), w_pc=w_pc[l].astype(bf16), w_out=w_out[l].astype(bf16),
                  norm_ffn=row(norm_ffn[l]), bd=bd, rw_hi=rw_hi, rw_lo=rw_lo, rb=rb)
        x1, h2, route = _merge(xf, y, g.reshape(n, d), bonus.reshape(n, d), yb, yc, h, wl[:, c2:], pm)

        blk_e, nv, nused, tok, dst, nblk = _dispatch(route, n)
        out2 = _moe(h2, blk_e, nv, nused, tok, dst, moe_w1[l].astype(bf16), moe_w3[l].astype(bf16),
                    moe_w2[l].astype(bf16), nblk, 2 * n)
        if l + 1 < depth:
            xf, h = _combine(x1, out2, route, norm_mix[l + 1], True)
        else:
            (out,) = _combine(x1, out2, route, norm_final, False)
    return out.reshape(bsz, T, d)
```
